```python
import math
import jax, jax.numpy as jnp
from jax import lax
import numpy as np

D_MODEL = 1024
BATCH = 8
SEQ = 2048
DEPTH = 1
DEC_BATCH = 128
DEC_SEQ = 8
PAST_LEN = 16384
PAGE_SIZE = 128

N_MEM = 256
D_MIX = D_MODEL
D_RET = D_MIX // 2
D_LRU = D_MIX - D_RET
RET_HEADS = 4
RET_HEAD_DIM = D_RET // RET_HEADS
RET_CHUNK = 128
ROPE_BASE = 10000.0
LRU_BLOCKS = 8
LRU_BLOCK_DIM = D_LRU // LRU_BLOCKS
CONV_WIDTH = 4
LRU_C = 8.0
XATTN_HEADS = 4
XATTN_HEAD_DIM = D_MODEL // XATTN_HEADS
D_FF = ((8 * D_MODEL // 3) + 127) // 128 * 128
D_IN = 4 * D_RET + 2 * D_LRU
EPS = 1e-6

kernel_name = 'hybrid_retention_rglru_macaron_decoder'


def rmsnorm(x, g):
    xf = x.astype(jnp.float32)
    y = xf * lax.rsqrt(jnp.mean(xf * xf, axis=-1, keepdims=True) + EPS)
    return (y * g.astype(jnp.float32)).astype(x.dtype)


def swiglu(x, wg, wu, wd):
    return (jax.nn.silu(x @ wg) * (x @ wu)) @ wd


def rotary(x, pos):
    half = x.shape[-1] // 2
    inv = ROPE_BASE ** (-jnp.arange(half, dtype=jnp.float32) / half)
    ang = pos.astype(jnp.float32)[:, None] * inv[None, :]
    cos = jnp.cos(ang)[None, :, None, :]
    sin = jnp.sin(ang)[None, :, None, :]
    x1, x2 = x[..., :half], x[..., half:]
    return jnp.concatenate([x1 * cos - x2 * sin, x1 * sin + x2 * cos], axis=-1)


def retention(q, k, v, s0):
    b, t, h, d = q.shape
    c = t if t <= RET_CHUNK else RET_CHUNK
    n_chunks = t // c
    lg = jnp.log(1.0 - 2.0 ** (-5.0 - jnp.arange(h, dtype=jnp.float32)))
    idx = jnp.arange(c, dtype=jnp.float32)
    diff = idx[:, None] - idx[None, :]
    dmask = jnp.where(diff[None] >= 0, jnp.exp(lg[:, None, None] * jnp.maximum(diff, 0.0)[None]), 0.0)
    q_dec = jnp.exp(lg[None, :] * (idx[:, None] + 1.0))
    k_dec = jnp.exp(lg[None, :] * (c - 1.0 - idx[:, None]))
    chunk_dec = jnp.exp(lg * c)

    def step(s, qkv):
        qc, kc, vc = qkv
        sc = jnp.einsum('bihd,bjhd->bhij', qc, kc) * dmask[None]
        o = jnp.einsum('bhij,bjhe->bihe', sc, vc) + jnp.einsum('bihd,bhde->bihe', qc, s) * q_dec[None, :, :, None]
        s = chunk_dec[None, :, None, None] * s + jnp.einsum('bjhd,bjhe->bhde', kc * k_dec[None, :, :, None], vc)
        return s, o

    to_chunks = lambda a: a.reshape(b, n_chunks, c, h, d).transpose(1, 0, 2, 3, 4)
    s_last, o = lax.scan(step, s0, (to_chunks(q), to_chunks(k), to_chunks(v)))
    o = o.transpose(1, 0, 2, 3, 4).reshape(b, t, h, d)
    return o, s_last


def causal_conv(u, buf, w, bias):
    t = u.shape[1]
    full = jnp.concatenate([buf.astype(u.dtype), u], axis=1)
    out = full[:, 0:t] * w[0]
    for j in range(1, CONV_WIDTH):
        out = out + full[:, j:j + t] * w[j]
    return out + bias, full[:, t:]


def rglru(xc, h0, wa, ba, wx, bx, lam):
    b, t, w = xc.shape
    xb = xc.reshape(b, t, LRU_BLOCKS, LRU_BLOCK_DIM)
    r = jax.nn.sigmoid(jnp.einsum('btnk,nkj->btnj', xb, wa.astype(jnp.float32)).reshape(b, t, w) + ba)
    i = jax.nn.sigmoid(jnp.einsum('btnk,nkj->btnj', xb, wx.astype(jnp.float32)).reshape(b, t, w) + bx)
    log_a = -LRU_C * r * jax.nn.softplus(-lam.astype(jnp.float32))
    a = jnp.exp(log_a)
    gx = jnp.sqrt(-jnp.expm1(2.0 * log_a)) * (i * xc)

    def step(h, ab):
        a_t, g_t = ab
        h = a_t * h + g_t
        return h, h

    h_last, hs = lax.scan(step, h0, (a.transpose(1, 0, 2), gx.transpose(1, 0, 2)))
    return hs.transpose(1, 0, 2), h_last


def mem_kv(mem, wk, wv):
    b, m, _ = mem.shape
    k = (mem @ wk).reshape(b, m, XATTN_HEADS, XATTN_HEAD_DIM)
    v = (mem @ wv).reshape(b, m, XATTN_HEADS, XATTN_HEAD_DIM)
    return k, v


def cross_attn(h, mk, mv, wq, wo):
    b, t, _ = h.shape
    q = (h @ wq).reshape(b, t, XATTN_HEADS, XATTN_HEAD_DIM)
    sc = jnp.einsum('bthd,bmhd->bhtm', q.astype(jnp.float32), mk.astype(jnp.float32)) * (XATTN_HEAD_DIM ** -0.5)
    pr = jax.nn.softmax(sc, axis=-1)
    o = jnp.einsum('bhtm,bmhd->bthd', pr, mv.astype(jnp.float32)).reshape(b, t, D_MODEL)
    return o.astype(h.dtype) @ wo


def layer(x, pos, s0, h0, conv0, mk, mv, p, l):
    b, t, _ = x.shape
    x = x + 0.5 * swiglu(rmsnorm(x, p['ffn1_norm'][l]), p['ffn1_wg'][l], p['ffn1_wu'][l], p['ffn1_wd'][l])
    hn = rmsnorm(x, p['mix_norm'][l])
    z = hn @ p['w_in'][l]
    q, k, v, g = (z[..., j * D_RET:(j + 1) * D_RET] for j in range(4))
    u = z[..., 4 * D_RET:4 * D_RET + D_LRU]
    gate = z[..., 4 * D_RET + D_LRU:]
    hd = (b, t, RET_HEADS, RET_HEAD_DIM)
    qf = rotary(q.reshape(hd).astype(jnp.float32), pos)
    kf = rotary(k.reshape(hd).astype(jnp.float32), pos) * (RET_HEAD_DIM ** -0.5)
    vf = v.reshape(hd).astype(jnp.float32)
    o, s_new = retention(qf, kf, vf, s0.astype(jnp.float32))
    mu = jnp.mean(o, axis=-1, keepdims=True)
    var = jnp.mean((o - mu) ** 2, axis=-1, keepdims=True)
    o = ((o - mu) * lax.rsqrt(var + EPS)).reshape(b, t, D_RET) * p['ret_gn_gain'][l].astype(jnp.float32)
    ret_out = (jax.nn.silu(g.astype(jnp.float32)) * o).astype(x.dtype)
    uc, conv_new = causal_conv(u, conv0, p['conv_w'][l], p['conv_b'][l])
    hs, h_new = rglru(uc.astype(jnp.float32), h0.astype(jnp.float32), p['lru_wa'][l], p['lru_ba'][l],
                      p['lru_wx'][l], p['lru_bx'][l], p['lru_lambda'][l])
    lru_out = (rmsnorm(hs, p['lru_norm'][l]) * jax.nn.gelu(gate.astype(jnp.float32))).astype(x.dtype)
    x = x + jnp.concatenate([ret_out, lru_out], axis=-1) @ p['w_out'][l]
    x = x + cross_attn(rmsnorm(x, p['xattn_norm'][l]), mk, mv, p['xattn_wq'][l], p['xattn_wo'][l])
    x = x + 0.5 * swiglu(rmsnorm(x, p['ffn2_norm'][l]), p['ffn2_wg'][l], p['ffn2_wu'][l], p['ffn2_wd'][l])
    return x, s_new, h_new, conv_new


def setup_inputs(seed: int = 0) -> dict:
    key = jax.random.key(seed)
    ks = jax.random.split(key, 40)
    nrm = lambda k, shape, scale: jax.random.normal(k, shape, jnp.float32) * scale
    gain = lambda k, shape: 1.0 + nrm(k, shape, 0.02)
    L = DEPTH
    u = jax.random.uniform(ks[20], (L, D_LRU), jnp.float32, 0.9, 0.999)
    s = u ** (1.0 / LRU_C)
    return {
        'x_prompt': nrm(ks[0], (BATCH, SEQ, D_MODEL), 1.0),
        'x_sample': nrm(ks[1], (DEC_BATCH, DEC_SEQ, D_MODEL), 1.0),
        'state_ret': nrm(ks[2], (L, DEC_BATCH, RET_HEADS, RET_HEAD_DIM, RET_HEAD_DIM), 0.1),
        'state_lru_h': nrm(ks[3], (L, DEC_BATCH, D_LRU), 0.5),
        'state_lru_conv': nrm(ks[4], (L, DEC_BATCH, CONV_WIDTH - 1, D_LRU), 1.0),
        'cache_mem_k': nrm(ks[5], (L, DEC_BATCH, N_MEM, XATTN_HEADS, XATTN_HEAD_DIM), 1.0),
        'cache_mem_v': nrm(ks[6], (L, DEC_BATCH, N_MEM, XATTN_HEADS, XATTN_HEAD_DIM), 1.0),
        'mem_prompt': nrm(ks[7], (BATCH, N_MEM, D_MODEL), 1.0),
        'ffn1_norm': gain(ks[8], (L, D_MODEL)),
        'ffn1_wg': nrm(ks[9], (L, D_MODEL, D_FF), D_MODEL ** -0.5),
        'ffn1_wu': nrm(ks[10], (L, D_MODEL, D_FF), D_MODEL ** -0.5),
        'ffn1_wd': nrm(ks[11], (L, D_FF, D_MODEL), D_FF ** -0.5),
        'mix_norm': gain(ks[12], (L, D_MODEL)),
        'w_in': nrm(ks[13], (L, D_MODEL, D_IN), D_MODEL ** -0.5),
        'ret_gn_gain': gain(ks[14], (L, D_RET)),
        'conv_w': nrm(ks[15], (L, CONV_WIDTH, D_LRU), CONV_WIDTH ** -0.5),
        'conv_b': nrm(ks[16], (L, D_LRU), 0.02),
        'lru_wa': nrm(ks[17], (L, LRU_BLOCKS, LRU_BLOCK_DIM, LRU_BLOCK_DIM), LRU_BLOCK_DIM ** -0.5),
        'lru_ba': nrm(ks[18], (L, D_LRU), 0.02),
        'lru_wx': nrm(ks[19], (L, LRU_BLOCKS, LRU_BLOCK_DIM, LRU_BLOCK_DIM), LRU_BLOCK_DIM ** -0.5),
        'lru_bx': nrm(ks[21], (L, D_LRU), 0.02),
        'lru_lambda': jnp.log(s) - jnp.log1p(-s),
        'lru_norm': gain(ks[22], (L, D_LRU)),
        'w_out': nrm(ks[23], (L, D_MIX, D_MODEL), D_MIX ** -0.5),
        'xattn_norm': gain(ks[24], (L, D_MODEL)),
        'xattn_wq': nrm(ks[25], (L, D_MODEL, D_MODEL), D_MODEL ** -0.5),
        'xattn_wk': nrm(ks[26], (L, D_MODEL, D_MODEL), D_MODEL ** -0.5),
        'xattn_wv': nrm(ks[27], (L, D_MODEL, D_MODEL), D_MODEL ** -0.5),
        'xattn_wo': nrm(ks[28], (L, D_MODEL, D_MODEL), D_MODEL ** -0.5),
        'ffn2_norm': gain(ks[29], (L, D_MODEL)),
        'ffn2_wg': nrm(ks[30], (L, D_MODEL, D_FF), D_MODEL ** -0.5),
        'ffn2_wu': nrm(ks[31], (L, D_MODEL, D_FF), D_MODEL ** -0.5),
        'ffn2_wd': nrm(ks[32], (L, D_FF, D_MODEL), D_FF ** -0.5),
        'final_norm': gain(ks[33], (D_MODEL,)),
    }


def reference(x_prompt, x_sample, state_ret, state_lru_h, state_lru_conv, cache_mem_k, cache_mem_v, mem_prompt,
              ffn1_norm, ffn1_wg, ffn1_wu, ffn1_wd, mix_norm, w_in, ret_gn_gain, conv_w, conv_b,
              lru_wa, lru_ba, lru_wx, lru_bx, lru_lambda, lru_norm, w_out,
              xattn_norm, xattn_wq, xattn_wk, xattn_wv, xattn_wo,
              ffn2_norm, ffn2_wg, ffn2_wu, ffn2_wd, final_norm):
    p = {'ffn1_norm': ffn1_norm, 'ffn1_wg': ffn1_wg, 'ffn1_wu': ffn1_wu, 'ffn1_wd': ffn1_wd,
         'mix_norm': mix_norm, 'w_in': w_in, 'ret_gn_gain': ret_gn_gain, 'conv_w': conv_w, 'conv_b': conv_b,
         'lru_wa': lru_wa, 'lru_ba': lru_ba, 'lru_wx': lru_wx, 'lru_bx': lru_bx, 'lru_lambda': lru_lambda,
         'lru_norm': lru_norm, 'w_out': w_out, 'xattn_norm': xattn_norm, 'xattn_wq': xattn_wq,
         'xattn_wo': xattn_wo, 'ffn2_norm': ffn2_norm, 'ffn2_wg': ffn2_wg, 'ffn2_wu': ffn2_wu, 'ffn2_wd': ffn2_wd}
    bp, tp, _ = x_prompt.shape
    ts = x_sample.shape[1]
    pos_p = jnp.arange(tp, dtype=jnp.int32)
    pos_s = PAST_LEN + jnp.arange(ts, dtype=jnp.int32)
    s0_p = jnp.zeros((bp, RET_HEADS, RET_HEAD_DIM, RET_HEAD_DIM), jnp.float32)
    h0_p = jnp.zeros((bp, D_LRU), jnp.float32)
    c0_p = jnp.zeros((bp, CONV_WIDTH - 1, D_LRU), x_prompt.dtype)
    yp, ys = x_prompt, x_sample
    p_ret, p_h, p_conv, p_mk, p_mv, s_ret, s_h, s_conv = [], [], [], [], [], [], [], []
    for l in range(DEPTH):
        mk, mv = mem_kv(mem_prompt, xattn_wk[l], xattn_wv[l])
        yp, sp, hp, cp = layer(yp, pos_p, s0_p, h0_p, c0_p, mk, mv, p, l)
        ys, ss, hs, cs = layer(ys, pos_s, state_ret[l], state_lru_h[l], state_lru_conv[l],
                               cache_mem_k[l], cache_mem_v[l], p, l)
        p_ret.append(sp); p_h.append(hp); p_conv.append(cp); p_mk.append(mk); p_mv.append(mv)
        s_ret.append(ss); s_h.append(hs); s_conv.append(cs)
    y_prompt = rmsnorm(yp, final_norm)
    y_sample = rmsnorm(ys, final_norm)
    return (y_prompt, y_sample, jnp.stack(p_ret), jnp.stack(p_h), jnp.stack(p_conv), jnp.stack(p_mk),
            jnp.stack(p_mv), jnp.stack(s_ret), jnp.stack(s_h), jnp.stack(s_conv))
```

```python
import functools
import math

import jax
import jax.numpy as jnp
from jax import lax
from jax.experimental import pallas as pl
from jax.experimental.pallas import tpu as pltpu

D_MODEL = 1024
PAST_LEN = 16384
N_MEM = 256
D_RET = 512
D_LRU = 512
RET_HEADS = 4
RET_HEAD_DIM = 128
RET_CHUNK = 128
ROPE_BASE = 10000.0
LRU_BLOCKS = 8
LRU_BLOCK_DIM = 64
CONV_WIDTH = 4
LRU_C = 8.0
XATTN_HEADS = 4
XATTN_HEAD_DIM = 256
D_FF = 2816
D_IN = 3072
EPS = 1e-6

SUBLANES = 8
VMEM_LIMIT_BYTES = 56 * 1024 * 1024

F32 = jnp.float32
BF16 = jnp.bfloat16


def _rms(x, g):
    ms = jnp.mean(x * x, axis=-1, keepdims=True)
    return x * lax.rsqrt(ms + EPS) * g


def _dot(a, b):
    return jnp.dot(a, b, preferred_element_type=F32)


def _swiglu(h_bf, wg_ref, wu_ref, wd_ref):
    g = _dot(h_bf, wg_ref[...])
    u = _dot(h_bf, wu_ref[...])
    a = (g * jax.nn.sigmoid(g)) * u
    return _dot(a.astype(BF16), wd_ref[...])


def _const_spec(shape):
    nd = len(shape)
    return pl.BlockSpec(shape, lambda *_: (0,) * nd)


def _params(n_axes):
    return pltpu.CompilerParams(dimension_semantics=("arbitrary",) * n_axes,
                                vmem_limit_bytes=VMEM_LIMIT_BYTES)


def _ffn_in_kernel(x_ref, n1_ref, wg_ref, wu_ref, wd_ref, n2_ref, win_ref, x1_ref, z_ref):
    x = x_ref[...]
    h = _rms(x, n1_ref[...]).astype(BF16)
    x1 = x + 0.5 * _swiglu(h, wg_ref, wu_ref, wd_ref)
    x1_ref[...] = x1
    hn = _rms(x1, n2_ref[...]).astype(BF16)
    z_ref[...] = _dot(hn, win_ref[...])


def _ffn_in(x, n1, wg, wu, wd, n2, win, tm):
    rows = x.shape[0]
    row_spec = lambda w: pl.BlockSpec((tm, w), lambda i: (i, 0))
    return pl.pallas_call(
        _ffn_in_kernel,
        grid=(rows // tm,),
        in_specs=[row_spec(D_MODEL), _const_spec(n1.shape), _const_spec(wg.shape), _const_spec(wu.shape),
                  _const_spec(wd.shape), _const_spec(n2.shape), _const_spec(win.shape)],
        out_specs=[row_spec(D_MODEL), row_spec(D_IN)],
        out_shape=[jax.ShapeDtypeStruct((rows, D_MODEL), F32), jax.ShapeDtypeStruct((rows, D_IN), F32)],
        compiler_params=_params(1),
        name="ffn_in",
    )(x, n1, wg, wu, wd, n2, win)


def _ffn_out_kernel(x_ref, n1_ref, wg_ref, wu_ref, wd_ref, nf_ref, y_ref):
    x = x_ref[...]
    h = _rms(x, n1_ref[...]).astype(BF16)
    x4 = x + 0.5 * _swiglu(h, wg_ref, wu_ref, wd_ref)
    y_ref[...] = _rms(x4, nf_ref[...])


def _ffn_out(x, n1, wg, wu, wd, nf, tm):
    rows = x.shape[0]
    row_spec = pl.BlockSpec((tm, D_MODEL), lambda i: (i, 0))
    return pl.pallas_call(
        _ffn_out_kernel,
        grid=(rows // tm,),
        in_specs=[row_spec, _const_spec(n1.shape), _const_spec(wg.shape), _const_spec(wu.shape),
                  _const_spec(wd.shape), _const_spec(nf.shape)],
        out_specs=row_spec,
        out_shape=jax.ShapeDtypeStruct((rows, D_MODEL), F32),
        compiler_params=_params(1),
        name="ffn_out",
    )(x, n1, wg, wu, wd, nf)


def _mem_kv_kernel(m_ref, wk_ref, wv_ref, k_ref, v_ref):
    m = m_ref[...].astype(BF16)
    k_ref[...] = _dot(m, wk_ref[...])
    v_ref[...] = _dot(m, wv_ref[...])


def _mem_kv(mem, wk, wv, tm):
    rows = mem.shape[0]
    row_spec = pl.BlockSpec((tm, D_MODEL), lambda i: (i, 0))
    return pl.pallas_call(
        _mem_kv_kernel,
        grid=(rows // tm,),
        in_specs=[row_spec, _const_spec(wk.shape), _const_spec(wv.shape)],
        out_specs=[row_spec, row_spec],
        out_shape=[jax.ShapeDtypeStruct((rows, D_MODEL), F32)] * 2,
        compiler_params=_params(1),
        name="mem_kv",
    )(mem, wk, wv)


def _rotary(x, cos2, sin2):
    return x * cos2 + pltpu.roll(x, RET_HEAD_DIM // 2, axis=1) * sin2


def _group_norm_gate(o, gain, g):
    mu = jnp.mean(o, axis=-1, keepdims=True)
    oc = o - mu
    var = jnp.mean(oc * oc, axis=-1, keepdims=True)
    on = oc * lax.rsqrt(var + EPS) * gain
    return (g * jax.nn.sigmoid(g)) * on


def _softplus(x):
    return jnp.maximum(x, 0.0) + jnp.log1p(jnp.exp(-jnp.abs(x)))


def _lru_gates(uc, wa_ref, ba_ref, wx_ref, bx_ref, lam_ref):
    ub = uc.astype(BF16)
    r = jax.nn.sigmoid(_dot(ub, wa_ref[...]) + ba_ref[...])
    i = jax.nn.sigmoid(_dot(ub, wx_ref[...]) + bx_ref[...])
    log_a = (-LRU_C) * r * _softplus(-lam_ref[...])
    a = jnp.exp(log_a)
    gx = jnp.sqrt(-jnp.tanh(log_a) * (a * a + 1.0)) * (i * uc)
    return a, gx


def _slab_scan(a, g):
    t = lax.broadcasted_iota(jnp.int32, a.shape, 1)
    d = 1
    while d < SUBLANES:
        keep = t >= d
        g_prev = jnp.where(keep, pltpu.roll(g, d, axis=1), 0.0)
        a_prev = jnp.where(keep, pltpu.roll(a, d, axis=1), 1.0)
        g = g + a * g_prev
        a = a * a_prev
        d *= 2
    return a, g


def _gelu_tanh(x):
    return 0.5 * x * (1.0 + jnp.tanh(math.sqrt(2.0 / math.pi) * (x + 0.044715 * (x * x * x))))


def _conv_taps(u3, hist3, cw_ref, cb_ref):
    t = lax.broadcasted_iota(jnp.int32, u3.shape, 1)
    acc = u3 * cw_ref[CONV_WIDTH - 1:CONV_WIDTH, :] + cb_ref[...]
    for k in range(1, CONV_WIDTH):
        prev = jnp.where(t >= k, pltpu.roll(u3, k, axis=1), pltpu.roll(hist3, k, axis=1))
        acc = acc + prev * cw_ref[CONV_WIDTH - 1 - k:CONV_WIDTH - k, :]
    return acc


def _cross_attention(q_bf, k_bf, v_bf):
    outs = []
    for h in range(XATTN_HEADS):
        sl = slice(h * XATTN_HEAD_DIM, (h + 1) * XATTN_HEAD_DIM)
        sc = lax.dot_general(q_bf[:, sl], k_bf[:, sl], (((1,), (1,)), ((), ())), preferred_element_type=F32)
        m = jnp.max(sc, axis=-1, keepdims=True)
        p = jnp.exp(sc - m)
        l = jnp.sum(p, axis=-1, keepdims=True)
        o = _dot(p.astype(BF16), v_bf[:, sl])
        outs.append((o / l).astype(BF16))
    return jnp.concatenate(outs, axis=-1)


def _mixer_prompt_kernel(z_ref, x1_ref, cs_ref, sn_ref, mk_ref, mv_ref, dmask_ref, qdec_ref, kdec_ref,
                         gn_ref, cw_ref, cb_ref, wa_ref, ba_ref, wx_ref, bx_ref, lam_ref, ln_ref,
                         wout_ref, xn_ref, wq_ref, wo_ref,
                         x3_ref, s_ref, h_ref, cv_ref,
                         mix_ref, hist_ref, *, tt, chunk_dec):
    t_idx = pl.program_id(1)

    @pl.when(t_idx == 0)
    def _():
        s_ref[...] = jnp.zeros_like(s_ref)
        h_ref[...] = jnp.zeros_like(h_ref)
        hist_ref[...] = jnp.zeros_like(hist_ref)

    for c in range(tt // RET_CHUNK):
        rows = slice(c * RET_CHUNK, (c + 1) * RET_CHUNK)
        cos2 = cs_ref[rows, :]
        sin2 = sn_ref[rows, :]
        for h in range(RET_HEADS):
            col = lambda j: slice(j * D_RET + h * RET_HEAD_DIM, j * D_RET + (h + 1) * RET_HEAD_DIM)
            q = _rotary(z_ref[0, rows, col(0)], cos2, sin2)
            k = _rotary(z_ref[0, rows, col(1)], cos2, sin2) * (RET_HEAD_DIM ** -0.5)
            v_bf = z_ref[0, rows, col(2)].astype(BF16)
            g = z_ref[0, rows, col(3)]
            q_bf = q.astype(BF16)
            sc = lax.dot_general(q_bf, k.astype(BF16), (((1,), (1,)), ((), ())), preferred_element_type=F32)
            sc = sc * dmask_ref[h]
            s_old = s_ref[0, h]
            o = _dot(sc.astype(BF16), v_bf) + _dot(q_bf, s_old.astype(BF16)) * qdec_ref[h]
            kd_bf = (k * kdec_ref[h]).astype(BF16)
            s_ref[0, h] = chunk_dec[h] * s_old + lax.dot_general(
                kd_bf, v_bf, (((0,), (0,)), ((), ())), preferred_element_type=F32)
            gain = gn_ref[:, h * RET_HEAD_DIM:(h + 1) * RET_HEAD_DIM]
            mix_ref[rows, h * RET_HEAD_DIM:(h + 1) * RET_HEAD_DIM] = _group_norm_gate(o, gain, g).astype(BF16)

    n_slab = tt // SUBLANES
    u = z_ref[0, :, 4 * D_RET:4 * D_RET + D_LRU]
    gate = z_ref[0, :, 4 * D_RET + D_LRU:]
    u3 = u.reshape(n_slab, SUBLANES, D_LRU)
    hist3 = jnp.concatenate([hist_ref[...].reshape(1, SUBLANES, D_LRU), u3[:-1]], axis=0)
    uc3 = _conv_taps(u3, hist3, cw_ref, cb_ref)
    hist_ref[...] = u3[n_slab - 1]
    cv_ref[0] = u3[n_slab - 1][SUBLANES - (CONV_WIDTH - 1):, :]
    a, gx = _lru_gates(uc3.reshape(tt, D_LRU), wa_ref, ba_ref, wx_ref, bx_ref, lam_ref)
    a_cum, h_loc = _slab_scan(a.reshape(n_slab, SUBLANES, D_LRU), gx.reshape(n_slab, SUBLANES, D_LRU))
    carry = h_ref[0]
    slabs = []
    for s in range(n_slab):
        hs_s = h_loc[s] + a_cum[s] * carry
        slabs.append(hs_s)
        carry = hs_s[SUBLANES - 1:, :]
    h_ref[0] = carry
    hs = jnp.concatenate(slabs, axis=0)
    mix_ref[:, D_RET:] = (_rms(hs, ln_ref[...]) * _gelu_tanh(gate)).astype(BF16)

    x2 = x1_ref[0] + _dot(mix_ref[...], wout_ref[...])
    hx = _rms(x2, xn_ref[...]).astype(BF16)
    q_bf = (_dot(hx, wq_ref[...]) * (XATTN_HEAD_DIM ** -0.5)).astype(BF16)
    o_bf = _cross_attention(q_bf, mk_ref[0].astype(BF16), mv_ref[0].astype(BF16))
    x3_ref[0] = x2 + _dot(o_bf, wo_ref[...])


def _mixer_prompt(z, x1, cos2, sin2, mk, mv, dmask, qdec, kdec, chunk_dec, small, wout, xn, wq, wo, tt):
    bsz, seq, _ = z.shape
    consts = [dmask, qdec, kdec, *small, wout, xn, wq, wo]
    seq_spec = lambda w: pl.BlockSpec((1, tt, w), lambda b, t: (b, t, 0))
    tab_spec = pl.BlockSpec((tt, RET_HEAD_DIM), lambda b, t: (t, 0))
    mem_spec = pl.BlockSpec((1, N_MEM, D_MODEL), lambda b, t: (b, 0, 0))
    return pl.pallas_call(
        functools.partial(_mixer_prompt_kernel, tt=tt, chunk_dec=chunk_dec),
        grid=(bsz, seq // tt),
        in_specs=[seq_spec(D_IN), seq_spec(D_MODEL), tab_spec, tab_spec, mem_spec, mem_spec]
        + [_const_spec(c.shape) for c in consts],
        out_specs=[seq_spec(D_MODEL),
                   pl.BlockSpec((1, RET_HEADS, RET_HEAD_DIM, RET_HEAD_DIM), lambda b, t: (b, 0, 0, 0)),
                   pl.BlockSpec((1, 1, D_LRU), lambda b, t: (b, 0, 0)),
                   pl.BlockSpec((1, CONV_WIDTH - 1, D_LRU), lambda b, t: (b, 0, 0))],
        out_shape=[jax.ShapeDtypeStruct((bsz, seq, D_MODEL), F32),
                   jax.ShapeDtypeStruct((bsz, RET_HEADS, RET_HEAD_DIM, RET_HEAD_DIM), F32),
                   jax.ShapeDtypeStruct((bsz, 1, D_LRU), F32),
                   jax.ShapeDtypeStruct((bsz, CONV_WIDTH - 1, D_LRU), F32)],
        scratch_shapes=[pltpu.VMEM((tt, D_MODEL), BF16), pltpu.VMEM((SUBLANES, D_LRU), F32)],
        compiler_params=_params(2),
        name="mixer_prompt",
    )(z, x1, cos2, sin2, mk, mv, *consts)


def _mixer_sample_kernel(z_ref, x1_ref, cs_ref, sn_ref, s0_ref, h0_ref, c0_ref, dmask_ref, qdec_ref, kdec_ref,
                         gn_ref, cw_ref, cb_ref, wa_ref, ba_ref, wx_ref, bx_ref, lam_ref, ln_ref,
                         wout_ref, xn_ref, wq_ref,
                         x2_ref, q_ref, s_ref, h_ref, cv_ref,
                         mix_ref, qs_ref, ks_ref, kd_ref, vs_ref, cross_ref, hist_ref, *, g_seq, chunk_dec):
    rows_n = g_seq * SUBLANES
    cos2 = cs_ref[...]
    sin2 = sn_ref[...]
    for h in range(RET_HEADS):
        col = lambda j: slice(j * D_RET + h * RET_HEAD_DIM, j * D_RET + (h + 1) * RET_HEAD_DIM)
        q = _rotary(z_ref[:, col(0)], cos2, sin2)
        k = _rotary(z_ref[:, col(1)], cos2, sin2) * (RET_HEAD_DIM ** -0.5)
        qs_ref[h] = q
        ks_ref[h] = k
        kd_ref[h] = k * kdec_ref[h]
        vs_ref[h] = z_ref[:, col(2)]

    def seq_body(s, carry):
        r0 = pl.multiple_of(s * SUBLANES, SUBLANES)
        for h in range(RET_HEADS):
            s_old = s0_ref[s, h]
            q_bf = qs_ref[h, pl.ds(r0, SUBLANES), :].astype(BF16)
            cross_ref[h, pl.ds(r0, SUBLANES), :] = _dot(q_bf, s_old.astype(BF16))
            kd_bf = kd_ref[h, pl.ds(r0, SUBLANES), :].astype(BF16)
            v_bf = vs_ref[h, pl.ds(r0, SUBLANES), :].astype(BF16)
            s_ref[s, h] = chunk_dec[h] * s_old + lax.dot_general(
                kd_bf, v_bf, (((0,), (0,)), ((), ())), preferred_element_type=F32)
        return carry

    lax.fori_loop(0, g_seq, seq_body, 0)

    for h in range(RET_HEADS):
        col_g = slice(3 * D_RET + h * RET_HEAD_DIM, 3 * D_RET + (h + 1) * RET_HEAD_DIM)
        q_bf = qs_ref[h].astype(BF16)
        k_bf = ks_ref[h].astype(BF16)
        sc = lax.dot_general(q_bf, k_bf, (((1,), (1,)), ((), ())), preferred_element_type=F32) * dmask_ref[h]
        o = _dot(sc.astype(BF16), vs_ref[h].astype(BF16)) + cross_ref[h] * qdec_ref[h]
        gain = gn_ref[:, h * RET_HEAD_DIM:(h + 1) * RET_HEAD_DIM]
        mix_ref[:, h * RET_HEAD_DIM:(h + 1) * RET_HEAD_DIM] = _group_norm_gate(o, gain, z_ref[:, col_g]).astype(BF16)

    u3 = z_ref[:, 4 * D_RET:4 * D_RET + D_LRU].reshape(g_seq, SUBLANES, D_LRU)
    gate = z_ref[:, 4 * D_RET + D_LRU:]
    hist_ref[...] = jnp.zeros_like(hist_ref)
    hist_ref[:, SUBLANES - (CONV_WIDTH - 1):, :] = c0_ref[...]
    uc3 = _conv_taps(u3, hist_ref[...], cw_ref, cb_ref)
    cv_ref[...] = u3[:, SUBLANES - (CONV_WIDTH - 1):, :]
    a, gx = _lru_gates(uc3.reshape(rows_n, D_LRU), wa_ref, ba_ref, wx_ref, bx_ref, lam_ref)
    a_cum, h_loc = _slab_scan(a.reshape(g_seq, SUBLANES, D_LRU), gx.reshape(g_seq, SUBLANES, D_LRU))
    hs3 = h_loc + a_cum * h0_ref[...]
    h_ref[...] = hs3[:, SUBLANES - 1:, :]
    mix_ref[:, D_RET:] = (_rms(hs3.reshape(rows_n, D_LRU), ln_ref[...]) * _gelu_tanh(gate)).astype(BF16)

    x2 = x1_ref[...] + _dot(mix_ref[...], wout_ref[...])
    x2_ref[...] = x2
    hx = _rms(x2, xn_ref[...]).astype(BF16)
    q_ref[...] = (_dot(hx, wq_ref[...]) * (XATTN_HEAD_DIM ** -0.5)).astype(BF16)


def _mixer_sample(z, x1, cos2, sin2, s0, h0, c0, dmask, qdec, kdec, chunk_dec, small, wout, xn, wq, g_seq):
    rows = z.shape[0]
    n_seq = rows // SUBLANES
    rn = g_seq * SUBLANES
    consts = [dmask, qdec, kdec, *small, wout, xn, wq]
    row_spec = lambda w: pl.BlockSpec((rn, w), lambda i: (i, 0))
    st_spec = pl.BlockSpec((g_seq, RET_HEADS, RET_HEAD_DIM, RET_HEAD_DIM), lambda i: (i, 0, 0, 0))
    h_spec = pl.BlockSpec((g_seq, 1, D_LRU), lambda i: (i, 0, 0))
    c_spec = pl.BlockSpec((g_seq, CONV_WIDTH - 1, D_LRU), lambda i: (i, 0, 0))
    head_buf = pltpu.VMEM((RET_HEADS, rn, RET_HEAD_DIM), F32)
    return pl.pallas_call(
        functools.partial(_mixer_sample_kernel, g_seq=g_seq, chunk_dec=chunk_dec),
        grid=(n_seq // g_seq,),
        in_specs=[row_spec(D_IN), row_spec(D_MODEL), _const_spec(cos2.shape), _const_spec(sin2.shape),
                  st_spec, h_spec, c_spec] + [_const_spec(c.shape) for c in consts],
        out_specs=[row_spec(D_MODEL), row_spec(D_MODEL), st_spec, h_spec, c_spec],
        out_shape=[jax.ShapeDtypeStruct((rows, D_MODEL), F32), jax.ShapeDtypeStruct((rows, D_MODEL), BF16),
                   jax.ShapeDtypeStruct(s0.shape, F32), jax.ShapeDtypeStruct(h0.shape, F32),
                   jax.ShapeDtypeStruct(c0.shape, F32)],
        scratch_shapes=[pltpu.VMEM((rn, D_MODEL), BF16), head_buf, head_buf, head_buf, head_buf, head_buf,
                        pltpu.VMEM((g_seq, SUBLANES, D_LRU), F32)],
        compiler_params=_params(1),
        name="mixer_sample",
    )(z, x1, cos2, sin2, s0, h0, c0, *consts)


def _xattn_sample_kernel(q_ref, x2_ref, k_ref, v_ref, wo_ref, x3_ref, o_ref, *, g_seq):
    for s in range(g_seq):
        rows = slice(s * SUBLANES, (s + 1) * SUBLANES)
        o_ref[rows, :] = _cross_attention(q_ref[rows, :], k_ref[s].astype(BF16), v_ref[s].astype(BF16))
    x3_ref[...] = x2_ref[...] + _dot(o_ref[...], wo_ref[...])


def _xattn_sample(q, x2, ck, cv, wo, g_seq):
    rows = q.shape[0]
    rn = g_seq * SUBLANES
    row_spec = pl.BlockSpec((rn, D_MODEL), lambda i: (i, 0))
    mem_spec = pl.BlockSpec((g_seq, N_MEM, D_MODEL), lambda i: (i, 0, 0))
    return pl.pallas_call(
        functools.partial(_xattn_sample_kernel, g_seq=g_seq),
        grid=(rows // rn,),
        in_specs=[row_spec, row_spec, mem_spec, mem_spec, _const_spec(wo.shape)],
        out_specs=row_spec,
        out_shape=jax.ShapeDtypeStruct((rows, D_MODEL), F32),
        scratch_shapes=[pltpu.VMEM((rn, D_MODEL), BF16)],
        compiler_params=_params(1),
        name="xattn_sample",
    )(q, x2, ck, cv, wo)


def _rope_tables(pos):
    half = RET_HEAD_DIM // 2
    inv = ROPE_BASE ** (-jnp.arange(half, dtype=F32) / half)
    ang = pos.astype(F32)[:, None] * inv[None, :]
    cos, sin = jnp.cos(ang), jnp.sin(ang)
    return jnp.concatenate([cos, cos], axis=-1), jnp.concatenate([-sin, sin], axis=-1)


def _decay_tables(c, n_rep):
    lg = jnp.log(1.0 - 2.0 ** (-5.0 - jnp.arange(RET_HEADS, dtype=F32)))
    idx = jnp.arange(c, dtype=F32)
    diff = idx[:, None] - idx[None, :]
    dmask = jnp.where(diff[None] >= 0, jnp.exp(lg[:, None, None] * jnp.maximum(diff, 0.0)[None]), 0.0)
    q_dec = jnp.exp(lg[:, None] * (idx[None, :] + 1.0))
    k_dec = jnp.exp(lg[:, None] * (c - 1.0 - idx[None, :]))
    n = c * n_rep
    if n_rep > 1:
        eye = jnp.eye(n_rep, dtype=F32)
        dmask = jnp.einsum('ab,hij->haibj', eye, dmask).reshape(RET_HEADS, n, n)
        q_dec = jnp.tile(q_dec, (1, n_rep))
        k_dec = jnp.tile(k_dec, (1, n_rep))
    bcast = lambda d: jnp.broadcast_to(d[:, :, None], (RET_HEADS, n, RET_HEAD_DIM))
    gammas = [1.0 - 2.0 ** (-5.0 - h) for h in range(RET_HEADS)]
    chunk_dec = tuple(g ** c for g in gammas)
    return dmask, bcast(q_dec), bcast(k_dec), chunk_dec


def _block_diag(w):
    n, k, j = w.shape
    eye = jnp.eye(n, dtype=w.dtype)
    return jnp.einsum('ab,akj->akbj', eye, w).reshape(n * k, n * j)


def kernel(x_prompt, x_sample, state_ret, state_lru_h, state_lru_conv, cache_mem_k, cache_mem_v, mem_prompt,
           ffn1_norm, ffn1_wg, ffn1_wu, ffn1_wd, mix_norm, w_in, ret_gn_gain, conv_w, conv_b,
           lru_wa, lru_ba, lru_wx, lru_bx, lru_lambda, lru_norm, w_out,
           xattn_norm, xattn_wq, xattn_wk, xattn_wv, xattn_wo,
           ffn2_norm, ffn2_wg, ffn2_wu, ffn2_wd, final_norm):
    depth = ffn1_wg.shape[0]
    assert depth == 1, "single-layer trunk"
    bp, tp, _ = x_prompt.shape
    bs, ts, _ = x_sample.shape
    assert ts == SUBLANES and tp % RET_CHUNK == 0
    bf = lambda w: w[0].astype(BF16)
    row = lambda v: v.reshape(1, -1)

    f1 = (row(ffn1_norm[0]), bf(ffn1_wg), bf(ffn1_wu), bf(ffn1_wd), row(mix_norm[0]), bf(w_in))
    f2 = (row(ffn2_norm[0]), bf(ffn2_wg), bf(ffn2_wu), bf(ffn2_wd), row(final_norm))
    small = [row(ret_gn_gain[0]), conv_w[0], row(conv_b[0]),
             _block_diag(lru_wa[0]).astype(BF16), row(lru_ba[0]),
             _block_diag(lru_wx[0]).astype(BF16), row(lru_bx[0]),
             row(lru_lambda[0]), row(lru_norm[0])]
    wout, xn, wq, wo = bf(w_out), row(xattn_norm[0]), bf(xattn_wq), bf(xattn_wo)

    tm = 256
    mk, mv = _mem_kv(mem_prompt.reshape(bp * N_MEM, D_MODEL), bf(xattn_wk), bf(xattn_wv), tm)
    x1p, zp = _ffn_in(x_prompt.reshape(bp * tp, D_MODEL), *f1, tm)
    cos_p, sin_p = _rope_tables(jnp.arange(tp, dtype=jnp.int32))
    dmask, qdec, kdec, cdec = _decay_tables(RET_CHUNK, 1)
    tt = 256
    x3p, s_p, h_p, c_p = _mixer_prompt(
        zp.reshape(bp, tp, D_IN), x1p.reshape(bp, tp, D_MODEL), cos_p, sin_p,
        mk.reshape(bp, N_MEM, D_MODEL), mv.reshape(bp, N_MEM, D_MODEL),
        dmask, qdec, kdec, cdec, small, wout, xn, wq, wo, tt)
    y_p = _ffn_out(x3p.reshape(bp * tp, D_MODEL), *f2, tm).reshape(bp, tp, D_MODEL)

    g_seq = 16
    x1s, zs = _ffn_in(x_sample.reshape(bs * ts, D_MODEL), *f1, tm)
    cos_s, sin_s = _rope_tables(PAST_LEN + jnp.arange(ts, dtype=jnp.int32))
    cos_s, sin_s = jnp.tile(cos_s, (g_seq, 1)), jnp.tile(sin_s, (g_seq, 1))
    dmask8, qdec8, kdec8, cdec8 = _decay_tables(ts, g_seq)
    x2s, qs, s_s, h_s, c_s = _mixer_sample(
        zs, x1s, cos_s, sin_s, state_ret[0], state_lru_h[0].reshape(bs, 1, D_LRU), state_lru_conv[0],
        dmask8, qdec8, kdec8, cdec8, small, wout, xn, wq, g_seq)
    g_att = 8
    x3s = _xattn_sample(qs, x2s, cache_mem_k[0].reshape(bs, N_MEM, D_MODEL),
                        cache_mem_v[0].reshape(bs, N_MEM, D_MODEL), wo, g_att)
    y_s = _ffn_out(x3s, *f2, tm).reshape(bs, ts, D_MODEL)

    kv_shape = (1, bp, N_MEM, XATTN_HEADS, XATTN_HEAD_DIM)
    return (y_p, y_s, s_p[None], h_p.reshape(1, bp, D_LRU), c_p[None],
            mk.reshape(kv_shape), mv.reshape(kv_shape), s_s[None], h_s.reshape(1, bs, D_LRU), c_s[None])
```

```python
import functools
import math

import jax
import jax.numpy as jnp
from jax import lax
from jax.experimental import pallas as pl
from jax.experimental.pallas import tpu as pltpu

D_MODEL = 1024
PAST_LEN = 16384
N_MEM = 256
D_RET = 512
D_LRU = 512
RET_HEADS = 4
RET_HEAD_DIM = 128
RET_CHUNK = 128
ROPE_BASE = 10000.0
LRU_BLOCKS = 8
LRU_BLOCK_DIM = 64
CONV_WIDTH = 4
LRU_C = 8.0
XATTN_HEADS = 4
XATTN_HEAD_DIM = 256
D_FF = 2816
D_IN = 3072
EPS = 1e-6

SUBLANES = 8
VMEM_LIMIT_BYTES = 56 * 1024 * 1024

F32 = jnp.float32
BF16 = jnp.bfloat16


def _rms(x, g):
    ms = jnp.mean(x * x, axis=-1, keepdims=True)
    return x * lax.rsqrt(ms + EPS) * g


def _dot(a, b):
    return jnp.dot(a, b, preferred_element_type=F32)


def _swiglu(h_bf, wg_ref, wu_ref, wd_ref):
    g = _dot(h_bf, wg_ref[...])
    u = _dot(h_bf, wu_ref[...])
    a = (g * jax.nn.sigmoid(g)) * u
    return _dot(a.astype(BF16), wd_ref[...])


def _const_spec(shape):
    nd = len(shape)
    return pl.BlockSpec(shape, lambda *_: (0,) * nd)


def _params(n_axes):
    return pltpu.CompilerParams(dimension_semantics=("arbitrary",) * n_axes,
                                vmem_limit_bytes=VMEM_LIMIT_BYTES)


def _ffn_in_kernel(x_ref, n1_ref, wg_ref, wu_ref, wd_ref, n2_ref, win_ref, x1_ref, z_ref):
    x = x_ref[...]
    h = _rms(x, n1_ref[...]).astype(BF16)
    x1 = x + 0.5 * _swiglu(h, wg_ref, wu_ref, wd_ref)
    x1_ref[...] = x1
    hn = _rms(x1, n2_ref[...]).astype(BF16)
    z_ref[...] = _dot(hn, win_ref[...])


def _ffn_in(x, n1, wg, wu, wd, n2, win, tm):
    rows = x.shape[0]
    row_spec = lambda w: pl.BlockSpec((tm, w), lambda i: (i, 0))
    return pl.pallas_call(
        _ffn_in_kernel,
        grid=(rows // tm,),
        in_specs=[row_spec(D_MODEL), _const_spec(n1.shape), _const_spec(wg.shape), _const_spec(wu.shape),
                  _const_spec(wd.shape), _const_spec(n2.shape), _const_spec(win.shape)],
        out_specs=[row_spec(D_MODEL), row_spec(D_IN)],
        out_shape=[jax.ShapeDtypeStruct((rows, D_MODEL), F32), jax.ShapeDtypeStruct((rows, D_IN), F32)],
        compiler_params=_params(1),
        name="ffn_in",
    )(x, n1, wg, wu, wd, n2, win)


def _ffn_out_kernel(x_ref, n1_ref, wg_ref, wu_ref, wd_ref, nf_ref, y_ref):
    x = x_ref[...]
    h = _rms(x, n1_ref[...]).astype(BF16)
    x4 = x + 0.5 * _swiglu(h, wg_ref, wu_ref, wd_ref)
    y_ref[...] = _rms(x4, nf_ref[...])


def _ffn_out(x, n1, wg, wu, wd, nf, tm):
    rows = x.shape[0]
    row_spec = pl.BlockSpec((tm, D_MODEL), lambda i: (i, 0))
    return pl.pallas_call(
        _ffn_out_kernel,
        grid=(rows // tm,),
        in_specs=[row_spec, _const_spec(n1.shape), _const_spec(wg.shape), _const_spec(wu.shape),
                  _const_spec(wd.shape), _const_spec(nf.shape)],
        out_specs=row_spec,
        out_shape=jax.ShapeDtypeStruct((rows, D_MODEL), F32),
        compiler_params=_params(1),
        name="ffn_out",
    )(x, n1, wg, wu, wd, nf)


def _mem_kv_kernel(m_ref, wk_ref, wv_ref, k_ref, v_ref, kb_ref, vb_ref):
    m = m_ref[0].astype(BF16)
    for w_ref, o_ref, ob_ref in ((wk_ref, k_ref, kb_ref), (wv_ref, v_ref, vb_ref)):
        kv = _dot(m, w_ref[...])
        ob_ref[0] = kv.astype(BF16)
        for h in range(XATTN_HEADS):
            o_ref[:, h, :] = kv[:, h * XATTN_HEAD_DIM:(h + 1) * XATTN_HEAD_DIM]


def _mem_kv(mem, wk, wv):
    bsz = mem.shape[0]
    row_spec = pl.BlockSpec((1, N_MEM, D_MODEL), lambda b: (b, 0, 0))
    cache_spec = pl.BlockSpec((None, None, N_MEM, XATTN_HEADS, XATTN_HEAD_DIM), lambda b: (0, b, 0, 0, 0))
    cache_shape = jax.ShapeDtypeStruct((1, bsz, N_MEM, XATTN_HEADS, XATTN_HEAD_DIM), F32)
    return pl.pallas_call(
        _mem_kv_kernel,
        grid=(bsz,),
        in_specs=[row_spec, _const_spec(wk.shape), _const_spec(wv.shape)],
        out_specs=[cache_spec, cache_spec, row_spec, row_spec],
        out_shape=[cache_shape, cache_shape] + [jax.ShapeDtypeStruct((bsz, N_MEM, D_MODEL), BF16)] * 2,
        compiler_params=_params(1),
        name="mem_kv",
    )(mem, wk, wv)


def _rotary(x, cos2, sin2):
    return x * cos2 + pltpu.roll(x, RET_HEAD_DIM // 2, axis=1) * sin2


def _group_norm_gate(o, gain, g):
    mu = jnp.mean(o, axis=-1, keepdims=True)
    oc = o - mu
    var = jnp.mean(oc * oc, axis=-1, keepdims=True)
    on = oc * lax.rsqrt(var + EPS) * gain
    return (g * jax.nn.sigmoid(g)) * on


def _softplus(x):
    return jnp.maximum(x, 0.0) + jnp.log1p(jnp.exp(-jnp.abs(x)))


def _lru_gates(uc, wa_ref, ba_ref, wx_ref, bx_ref, lam_ref):
    ub = uc.astype(BF16)
    r = jax.nn.sigmoid(_dot(ub, wa_ref[...]) + ba_ref[...])
    i = jax.nn.sigmoid(_dot(ub, wx_ref[...]) + bx_ref[...])
    log_a = (-LRU_C) * r * _softplus(-lam_ref[...])
    a = jnp.exp(log_a)
    gx = jnp.sqrt(-jnp.tanh(log_a) * (a * a + 1.0)) * (i * uc)
    return a, gx


def _slab_scan(a, g):
    t = lax.broadcasted_iota(jnp.int32, a.shape, 1)
    d = 1
    while d < SUBLANES:
        keep = t >= d
        g_prev = jnp.where(keep, pltpu.roll(g, d, axis=1), 0.0)
        a_prev = jnp.where(keep, pltpu.roll(a, d, axis=1), 1.0)
        g = g + a * g_prev
        a = a * a_prev
        d *= 2
    return a, g


def _gelu_tanh(x):
    return 0.5 * x * (1.0 + jnp.tanh(math.sqrt(2.0 / math.pi) * (x + 0.044715 * (x * x * x))))


def _conv_taps(u3, hist3, cw_ref, cb_ref):
    t = lax.broadcasted_iota(jnp.int32, u3.shape, 1)
    acc = u3 * cw_ref[CONV_WIDTH - 1:CONV_WIDTH, :] + cb_ref[...]
    for k in range(1, CONV_WIDTH):
        prev = jnp.where(t >= k, pltpu.roll(u3, k, axis=1), pltpu.roll(hist3, k, axis=1))
        acc = acc + prev * cw_ref[CONV_WIDTH - 1 - k:CONV_WIDTH - k, :]
    return acc


def _cross_attention(q_bf, k_bf, v_bf):
    outs = []
    for h in range(XATTN_HEADS):
        sl = slice(h * XATTN_HEAD_DIM, (h + 1) * XATTN_HEAD_DIM)
        sc = lax.dot_general(q_bf[:, sl], k_bf[:, sl], (((1,), (1,)), ((), ())), preferred_element_type=F32)
        m = jnp.max(sc, axis=-1, keepdims=True)
        p = jnp.exp(sc - m)
        l = jnp.sum(p, axis=-1, keepdims=True)
        o = _dot(p.astype(BF16), v_bf[:, sl])
        outs.append((o / l).astype(BF16))
    return jnp.concatenate(outs, axis=-1)


def _mixer_prompt_kernel(z_ref, x1_ref, cs_ref, sn_ref, mk_ref, mv_ref, dmask_ref, qdec_ref, kdec_ref,
                         gn_ref, cw_ref, cb_ref, wa_ref, ba_ref, wx_ref, bx_ref, lam_ref, ln_ref,
                         wout_ref, xn_ref, wq_ref, wo_ref,
                         x3_ref, s_ref, h_ref, cv_ref,
                         mix_ref, hist_ref, *, tt, chunk_dec):
    t_idx = pl.program_id(1)

    @pl.when(t_idx == 0)
    def _():
        s_ref[...] = jnp.zeros_like(s_ref)
        h_ref[...] = jnp.zeros_like(h_ref)
        hist_ref[...] = jnp.zeros_like(hist_ref)

    for c in range(tt // RET_CHUNK):
        rows = slice(c * RET_CHUNK, (c + 1) * RET_CHUNK)
        cos2 = cs_ref[rows, :]
        sin2 = sn_ref[rows, :]
        for h in range(RET_HEADS):
            col = lambda j: slice(j * D_RET + h * RET_HEAD_DIM, j * D_RET + (h + 1) * RET_HEAD_DIM)
            q = _rotary(z_ref[0, rows, col(0)], cos2, sin2)
            k = _rotary(z_ref[0, rows, col(1)], cos2, sin2) * (RET_HEAD_DIM ** -0.5)
            v_bf = z_ref[0, rows, col(2)].astype(BF16)
            g = z_ref[0, rows, col(3)]
            q_bf = q.astype(BF16)
            sc = lax.dot_general(q_bf, k.astype(BF16), (((1,), (1,)), ((), ())), preferred_element_type=F32)
            sc = sc * dmask_ref[h]
            s_old = s_ref[0, h]
            o = _dot(sc.astype(BF16), v_bf) + _dot(q_bf, s_old.astype(BF16)) * qdec_ref[h]
            kd_bf = (k * kdec_ref[h]).astype(BF16)
            s_ref[0, h] = chunk_dec[h] * s_old + lax.dot_general(
                kd_bf, v_bf, (((0,), (0,)), ((), ())), preferred_element_type=F32)
            gain = gn_ref[:, h * RET_HEAD_DIM:(h + 1) * RET_HEAD_DIM]
            mix_ref[rows, h * RET_HEAD_DIM:(h + 1) * RET_HEAD_DIM] = _group_norm_gate(o, gain, g).astype(BF16)

    n_slab = tt // SUBLANES
    u = z_ref[0, :, 4 * D_RET:4 * D_RET + D_LRU]
    gate = z_ref[0, :, 4 * D_RET + D_LRU:]
    u3 = u.reshape(n_slab, SUBLANES, D_LRU)
    hist3 = jnp.concatenate([hist_ref[...].reshape(1, SUBLANES, D_LRU), u3[:-1]], axis=0)
    uc3 = _conv_taps(u3, hist3, cw_ref, cb_ref)
    hist_ref[...] = u3[n_slab - 1]
    cv_ref[0] = u3[n_slab - 1][SUBLANES - (CONV_WIDTH - 1):, :]
    a, gx = _lru_gates(uc3.reshape(tt, D_LRU), wa_ref, ba_ref, wx_ref, bx_ref, lam_ref)
    a_cum, h_loc = _slab_scan(a.reshape(n_slab, SUBLANES, D_LRU), gx.reshape(n_slab, SUBLANES, D_LRU))
    carry = h_ref[0]
    slabs = []
    for s in range(n_slab):
        hs_s = h_loc[s] + a_cum[s] * carry
        slabs.append(hs_s)
        carry = hs_s[SUBLANES - 1:, :]
    h_ref[0] = carry
    hs = jnp.concatenate(slabs, axis=0)
    mix_ref[:, D_RET:] = (_rms(hs, ln_ref[...]) * _gelu_tanh(gate)).astype(BF16)

    x2 = x1_ref[0] + _dot(mix_ref[...], wout_ref[...])
    hx = _rms(x2, xn_ref[...]).astype(BF16)
    q_bf = (_dot(hx, wq_ref[...]) * (XATTN_HEAD_DIM ** -0.5)).astype(BF16)
    o_bf = _cross_attention(q_bf, mk_ref[0], mv_ref[0])
    x3_ref[0] = x2 + _dot(o_bf, wo_ref[...])


def _mixer_prompt(z, x1, cos2, sin2, mk, mv, dmask, qdec, kdec, chunk_dec, small, wout, xn, wq, wo, tt):
    bsz, seq, _ = z.shape
    consts = [dmask, qdec, kdec, *small, wout, xn, wq, wo]
    seq_spec = lambda w: pl.BlockSpec((1, tt, w), lambda b, t: (b, t, 0))
    tab_spec = pl.BlockSpec((tt, RET_HEAD_DIM), lambda b, t: (t, 0))
    mem_spec = pl.BlockSpec((1, N_MEM, D_MODEL), lambda b, t: (b, 0, 0))
    return pl.pallas_call(
        functools.partial(_mixer_prompt_kernel, tt=tt, chunk_dec=chunk_dec),
        grid=(bsz, seq // tt),
        in_specs=[seq_spec(D_IN), seq_spec(D_MODEL), tab_spec, tab_spec, mem_spec, mem_spec]
        + [_const_spec(c.shape) for c in consts],
        out_specs=[seq_spec(D_MODEL),
                   pl.BlockSpec((1, RET_HEADS, RET_HEAD_DIM, RET_HEAD_DIM), lambda b, t: (b, 0, 0, 0)),
                   pl.BlockSpec((1, 1, D_LRU), lambda b, t: (b, 0, 0)),
                   pl.BlockSpec((1, CONV_WIDTH - 1, D_LRU), lambda b, t: (b, 0, 0))],
        out_shape=[jax.ShapeDtypeStruct((bsz, seq, D_MODEL), F32),
                   jax.ShapeDtypeStruct((bsz, RET_HEADS, RET_HEAD_DIM, RET_HEAD_DIM), F32),
                   jax.ShapeDtypeStruct((bsz, 1, D_LRU), F32),
                   jax.ShapeDtypeStruct((bsz, CONV_WIDTH - 1, D_LRU), F32)],
        scratch_shapes=[pltpu.VMEM((tt, D_MODEL), BF16), pltpu.VMEM((SUBLANES, D_LRU), F32)],
        compiler_params=_params(2),
        name="mixer_prompt",
    )(z, x1, cos2, sin2, mk, mv, *consts)


def _mixer_sample_kernel(z_ref, x1_ref, cs_ref, sn_ref, s0_ref, h0_ref, c0_ref, dmask_ref, qdec_ref, kdec_ref,
                         gn_ref, cw_ref, cb_ref, wa_ref, ba_ref, wx_ref, bx_ref, lam_ref, ln_ref,
                         wout_ref, xn_ref, wq_ref,
                         x2_ref, q_ref, s_ref, h_ref, cv_ref,
                         mix_ref, qs_ref, ks_ref, kd_ref, vs_ref, cross_ref, hist_ref, *, g_seq, chunk_dec):
    rows_n = g_seq * SUBLANES
    cos2 = cs_ref[...]
    sin2 = sn_ref[...]
    for h in range(RET_HEADS):
        col = lambda j: slice(j * D_RET + h * RET_HEAD_DIM, j * D_RET + (h + 1) * RET_HEAD_DIM)
        q = _rotary(z_ref[:, col(0)], cos2, sin2)
        k = _rotary(z_ref[:, col(1)], cos2, sin2) * (RET_HEAD_DIM ** -0.5)
        qs_ref[h] = q
        ks_ref[h] = k
        kd_ref[h] = k * kdec_ref[h]
        vs_ref[h] = z_ref[:, col(2)]

    def seq_body(s, carry):
        r0 = pl.multiple_of(s * SUBLANES, SUBLANES)
        for h in range(RET_HEADS):
            s_old = s0_ref[s, h]
            q_bf = qs_ref[h, pl.ds(r0, SUBLANES), :].astype(BF16)
            cross_ref[h, pl.ds(r0, SUBLANES), :] = _dot(q_bf, s_old.astype(BF16))
            kd_bf = kd_ref[h, pl.ds(r0, SUBLANES), :].astype(BF16)
            v_bf = vs_ref[h, pl.ds(r0, SUBLANES), :].astype(BF16)
            s_ref[s, h] = chunk_dec[h] * s_old + lax.dot_general(
                kd_bf, v_bf, (((0,), (0,)), ((), ())), preferred_element_type=F32)
        return carry

    lax.fori_loop(0, g_seq, seq_body, 0)

    for h in range(RET_HEADS):
        col_g = slice(3 * D_RET + h * RET_HEAD_DIM, 3 * D_RET + (h + 1) * RET_HEAD_DIM)
        q_bf = qs_ref[h].astype(BF16)
        k_bf = ks_ref[h].astype(BF16)
        sc = lax.dot_general(q_bf, k_bf, (((1,), (1,)), ((), ())), preferred_element_type=F32) * dmask_ref[h]
        o = _dot(sc.astype(BF16), vs_ref[h].astype(BF16)) + cross_ref[h] * qdec_ref[h]
        gain = gn_ref[:, h * RET_HEAD_DIM:(h + 1) * RET_HEAD_DIM]
        mix_ref[:, h * RET_HEAD_DIM:(h + 1) * RET_HEAD_DIM] = _group_norm_gate(o, gain, z_ref[:, col_g]).astype(BF16)

    u3 = z_ref[:, 4 * D_RET:4 * D_RET + D_LRU].reshape(g_seq, SUBLANES, D_LRU)
    gate = z_ref[:, 4 * D_RET + D_LRU:]
    hist_ref[...] = jnp.zeros_like(hist_ref)
    hist_ref[:, SUBLANES - (CONV_WIDTH - 1):, :] = c0_ref[...]
    uc3 = _conv_taps(u3, hist_ref[...], cw_ref, cb_ref)
    cv_ref[...] = u3[:, SUBLANES - (CONV_WIDTH - 1):, :]
    a, gx = _lru_gates(uc3.reshape(rows_n, D_LRU), wa_ref, ba_ref, wx_ref, bx_ref, lam_ref)
    a_cum, h_loc = _slab_scan(a.reshape(g_seq, SUBLANES, D_LRU), gx.reshape(g_seq, SUBLANES, D_LRU))
    hs3 = h_loc + a_cum * h0_ref[...]
    h_ref[...] = hs3[:, SUBLANES - 1:, :]
    mix_ref[:, D_RET:] = (_rms(hs3.reshape(rows_n, D_LRU), ln_ref[...]) * _gelu_tanh(gate)).astype(BF16)

    x2 = x1_ref[...] + _dot(mix_ref[...], wout_ref[...])
    x2_ref[...] = x2
    hx = _rms(x2, xn_ref[...]).astype(BF16)
    q = (_dot(hx, wq_ref[...]) * (XATTN_HEAD_DIM ** -0.5)).reshape(g_seq, SUBLANES, D_MODEL)
    q_heads = [q[:, :, h * XATTN_HEAD_DIM:(h + 1) * XATTN_HEAD_DIM] for h in range(XATTN_HEADS)]
    q_ref[...] = jnp.concatenate(q_heads, axis=1).reshape(rows_n * XATTN_HEADS, XATTN_HEAD_DIM).astype(BF16)


def _mixer_sample(z, x1, cos2, sin2, s0, h0, c0, dmask, qdec, kdec, chunk_dec, small, wout, xn, wq, g_seq):
    rows = z.shape[0]
    n_seq = rows // SUBLANES
    rn = g_seq * SUBLANES
    consts = [dmask, qdec, kdec, *small, wout, xn, wq]
    row_spec = lambda w: pl.BlockSpec((rn, w), lambda i: (i, 0))
    st_spec = pl.BlockSpec((g_seq, RET_HEADS, RET_HEAD_DIM, RET_HEAD_DIM), lambda i: (i, 0, 0, 0))
    h_spec = pl.BlockSpec((g_seq, 1, D_LRU), lambda i: (i, 0, 0))
    c_spec = pl.BlockSpec((g_seq, CONV_WIDTH - 1, D_LRU), lambda i: (i, 0, 0))
    head_buf = pltpu.VMEM((RET_HEADS, rn, RET_HEAD_DIM), F32)
    return pl.pallas_call(
        functools.partial(_mixer_sample_kernel, g_seq=g_seq, chunk_dec=chunk_dec),
        grid=(n_seq // g_seq,),
        in_specs=[row_spec(D_IN), row_spec(D_MODEL), _const_spec(cos2.shape), _const_spec(sin2.shape),
                  st_spec, h_spec, c_spec] + [_const_spec(c.shape) for c in consts],
        out_specs=[row_spec(D_MODEL), pl.BlockSpec((rn * XATTN_HEADS, XATTN_HEAD_DIM), lambda i: (i, 0)),
                   st_spec, h_spec, c_spec],
        out_shape=[jax.ShapeDtypeStruct((rows, D_MODEL), F32),
                   jax.ShapeDtypeStruct((rows * XATTN_HEADS, XATTN_HEAD_DIM), BF16),
                   jax.ShapeDtypeStruct(s0.shape, F32), jax.ShapeDtypeStruct(h0.shape, F32),
                   jax.ShapeDtypeStruct(c0.shape, F32)],
        scratch_shapes=[pltpu.VMEM((rn, D_MODEL), BF16), head_buf, head_buf, head_buf, head_buf, head_buf,
                        pltpu.VMEM((g_seq, SUBLANES, D_LRU), F32)],
        compiler_params=_params(1),
        name="mixer_sample",
    )(z, x1, cos2, sin2, s0, h0, c0, *consts)


def _xattn_sample_kernel(q_ref, x2_ref, k_ref, v_ref, wo_ref, x3_ref, o_ref, *, g_seq):
    n_q = XATTN_HEADS * SUBLANES
    n_kv = N_MEM * XATTN_HEADS
    col_head = lax.broadcasted_iota(jnp.int32, (n_q, n_kv), 1) % XATTN_HEADS
    row_head = lax.broadcasted_iota(jnp.int32, (n_q, n_kv), 0) // SUBLANES
    own_head = col_head == row_head
    for s in range(g_seq):
        k_all = k_ref[s].reshape(n_kv, XATTN_HEAD_DIM).astype(BF16)
        v_all = v_ref[s].reshape(n_kv, XATTN_HEAD_DIM).astype(BF16)
        sc = lax.dot_general(q_ref[s * n_q:(s + 1) * n_q, :], k_all, (((1,), (1,)), ((), ())),
                             preferred_element_type=F32)
        sc = jnp.where(own_head, sc, -jnp.inf)
        p = jnp.exp(sc - jnp.max(sc, axis=-1, keepdims=True))
        o = _dot(p.astype(BF16), v_all) / jnp.sum(p, axis=-1, keepdims=True)
        for h in range(XATTN_HEADS):
            o_ref[s * SUBLANES:(s + 1) * SUBLANES, h * XATTN_HEAD_DIM:(h + 1) * XATTN_HEAD_DIM] = (
                o[h * SUBLANES:(h + 1) * SUBLANES, :])
    x3_ref[...] = x2_ref[...] + _dot(o_ref[...].astype(BF16), wo_ref[...])


def _xattn_sample(q, x2, ck, cv, wo, g_seq):
    rows = x2.shape[0]
    rn = g_seq * SUBLANES
    row_spec = pl.BlockSpec((rn, D_MODEL), lambda i: (i, 0))
    q_spec = pl.BlockSpec((rn * XATTN_HEADS, XATTN_HEAD_DIM), lambda i: (i, 0))
    mem_spec = pl.BlockSpec((None, g_seq, N_MEM, XATTN_HEADS, XATTN_HEAD_DIM), lambda i: (0, i, 0, 0, 0))
    return pl.pallas_call(
        functools.partial(_xattn_sample_kernel, g_seq=g_seq),
        grid=(rows // rn,),
        in_specs=[q_spec, row_spec, mem_spec, mem_spec, _const_spec(wo.shape)],
        out_specs=row_spec,
        out_shape=jax.ShapeDtypeStruct((rows, D_MODEL), F32),
        scratch_shapes=[pltpu.VMEM((rn, D_MODEL), F32)],
        compiler_params=_params(1),
        name="xattn_sample",
    )(q, x2, ck, cv, wo)


def _rope_tables(pos):
    half = RET_HEAD_DIM // 2
    inv = ROPE_BASE ** (-jnp.arange(half, dtype=F32) / half)
    ang = pos.astype(F32)[:, None] * inv[None, :]
    cos, sin = jnp.cos(ang), jnp.sin(ang)
    return jnp.concatenate([cos, cos], axis=-1), jnp.concatenate([-sin, sin], axis=-1)


def _decay_tables(c, n_rep):
    lg = jnp.log(1.0 - 2.0 ** (-5.0 - jnp.arange(RET_HEADS, dtype=F32)))
    idx = jnp.arange(c, dtype=F32)
    diff = idx[:, None] - idx[None, :]
    dmask = jnp.where(diff[None] >= 0, jnp.exp(lg[:, None, None] * jnp.maximum(diff, 0.0)[None]), 0.0)
    q_dec = jnp.exp(lg[:, None] * (idx[None, :] + 1.0))
    k_dec = jnp.exp(lg[:, None] * (c - 1.0 - idx[None, :]))
    n = c * n_rep
    if n_rep > 1:
        eye = jnp.eye(n_rep, dtype=F32)
        dmask = jnp.einsum('ab,hij->haibj', eye, dmask).reshape(RET_HEADS, n, n)
        q_dec = jnp.tile(q_dec, (1, n_rep))
        k_dec = jnp.tile(k_dec, (1, n_rep))
    bcast = lambda d: jnp.broadcast_to(d[:, :, None], (RET_HEADS, n, RET_HEAD_DIM))
    gammas = [1.0 - 2.0 ** (-5.0 - h) for h in range(RET_HEADS)]
    chunk_dec = tuple(g ** c for g in gammas)
    return dmask, bcast(q_dec), bcast(k_dec), chunk_dec


def _block_diag(w):
    n, k, j = w.shape
    eye = jnp.eye(n, dtype=w.dtype)
    return jnp.einsum('ab,akj->akbj', eye, w).reshape(n * k, n * j)


def kernel(x_prompt, x_sample, state_ret, state_lru_h, state_lru_conv, cache_mem_k, cache_mem_v, mem_prompt,
           ffn1_norm, ffn1_wg, ffn1_wu, ffn1_wd, mix_norm, w_in, ret_gn_gain, conv_w, conv_b,
           lru_wa, lru_ba, lru_wx, lru_bx, lru_lambda, lru_norm, w_out,
           xattn_norm, xattn_wq, xattn_wk, xattn_wv, xattn_wo,
           ffn2_norm, ffn2_wg, ffn2_wu, ffn2_wd, final_norm):
    depth = ffn1_wg.shape[0]
    assert depth == 1, "single-layer trunk"
    bp, tp, _ = x_prompt.shape
    bs, ts, _ = x_sample.shape
    assert ts == SUBLANES and tp % RET_CHUNK == 0
    bf = lambda w: w[0].astype(BF16)
    row = lambda v: v.reshape(1, -1)

    f1 = (row(ffn1_norm[0]), bf(ffn1_wg), bf(ffn1_wu), bf(ffn1_wd), row(mix_norm[0]), bf(w_in))
    f2 = (row(ffn2_norm[0]), bf(ffn2_wg), bf(ffn2_wu), bf(ffn2_wd), row(final_norm))
    small = [row(ret_gn_gain[0]), conv_w[0], row(conv_b[0]),
             _block_diag(lru_wa[0]).astype(BF16), row(lru_ba[0]),
             _block_diag(lru_wx[0]).astype(BF16), row(lru_bx[0]),
             row(lru_lambda[0]), row(lru_norm[0])]
    wout, xn, wq, wo = bf(w_out), row(xattn_norm[0]), bf(xattn_wq), bf(xattn_wo)

    tm = 256
    mk, mv, mk_bf, mv_bf = _mem_kv(mem_prompt, bf(xattn_wk), bf(xattn_wv))
    x1p, zp = _ffn_in(x_prompt.reshape(bp * tp, D_MODEL), *f1, tm)
    cos_p, sin_p = _rope_tables(jnp.arange(tp, dtype=jnp.int32))
    dmask, qdec, kdec, cdec = _decay_tables(RET_CHUNK, 1)
    tt = 256
    x3p, s_p, h_p, c_p = _mixer_prompt(
        zp.reshape(bp, tp, D_IN), x1p.reshape(bp, tp, D_MODEL), cos_p, sin_p,
        mk_bf, mv_bf, dmask, qdec, kdec, cdec, small, wout, xn, wq, wo, tt)
    y_p = _ffn_out(x3p.reshape(bp * tp, D_MODEL), *f2, tm).reshape(bp, tp, D_MODEL)

    g_seq = 16
    x1s, zs = _ffn_in(x_sample.reshape(bs * ts, D_MODEL), *f1, tm)
    cos_s, sin_s = _rope_tables(PAST_LEN + jnp.arange(ts, dtype=jnp.int32))
    cos_s, sin_s = jnp.tile(cos_s, (g_seq, 1)), jnp.tile(sin_s, (g_seq, 1))
    dmask8, qdec8, kdec8, cdec8 = _decay_tables(ts, g_seq)
    x2s, qs, s_s, h_s, c_s = _mixer_sample(
        zs, x1s, cos_s, sin_s, state_ret[0], state_lru_h[0].reshape(bs, 1, D_LRU), state_lru_conv[0],
        dmask8, qdec8, kdec8, cdec8, small, wout, xn, wq, g_seq)
    g_att = 8
    x3s = _xattn_sample(qs, x2s, cache_mem_k, cache_mem_v, wo, g_att)
    y_s = _ffn_out(x3s, *f2, tm).reshape(bs, ts, D_MODEL)

    return (y_p, y_s, s_p[None], h_p.reshape(1, bp, D_LRU), c_p[None],
            mk, mv, s_s[None], h_s.reshape(1, bs, D_LRU), c_s[None])
```

```python
import functools
import math

import jax
import jax.numpy as jnp
from jax import lax
from jax.experimental import pallas as pl
from jax.experimental.pallas import tpu as pltpu

D_MODEL = 1024
PAST_LEN = 16384
N_MEM = 256
D_RET = 512
D_LRU = 512
RET_HEADS = 4
RET_HEAD_DIM = 128
RET_CHUNK = 128
ROPE_BASE = 10000.0
LRU_BLOCKS = 8
LRU_BLOCK_DIM = 64
CONV_WIDTH = 4
LRU_C = 8.0
XATTN_HEADS = 4
XATTN_HEAD_DIM = 256
D_FF = 2816
D_IN = 3072
EPS = 1e-6

SUBLANES = 8
FFN_CHUNK = 256
LRU_GROUP = 128
VMEM_LIMIT_BYTES = 56 * 1024 * 1024

F32 = jnp.float32
BF16 = jnp.bfloat16


def _rms(x, g):
    ms = jnp.mean(x * x, axis=-1, keepdims=True)
    return x * lax.rsqrt(ms + EPS) * g


def _dot(a, b):
    return jnp.dot(a, b, preferred_element_type=F32)


def _swiglu(h_bf, wg_ref, wu_ref, wd_ref):
    g = _dot(h_bf, wg_ref[...])
    u = _dot(h_bf, wu_ref[...])
    a = (g * jax.nn.sigmoid(g)) * u
    return _dot(a.astype(BF16), wd_ref[...])


def _const_spec(shape):
    nd = len(shape)
    return pl.BlockSpec(shape, lambda *_: (0,) * nd)


def _params(n_axes):
    return pltpu.CompilerParams(dimension_semantics=("arbitrary",) * n_axes,
                                vmem_limit_bytes=VMEM_LIMIT_BYTES)


def _ffn_in_kernel(x_ref, n1_ref, wg_ref, wu_ref, wd_ref, n2_ref, win_ref, x1_ref, z_ref):
    x = x_ref[...]
    h = _rms(x, n1_ref[...]).astype(BF16)
    x1 = x + 0.5 * _swiglu(h, wg_ref, wu_ref, wd_ref)
    x1_ref[...] = x1
    hn = _rms(x1, n2_ref[...]).astype(BF16)
    z_ref[...] = _dot(hn, win_ref[...])


def _ffn_in(x, n1, wg, wu, wd, n2, win, tm):
    rows = x.shape[0]
    row_spec = lambda w: pl.BlockSpec((tm, w), lambda i: (i, 0))
    return pl.pallas_call(
        _ffn_in_kernel,
        grid=(rows // tm,),
        in_specs=[row_spec(D_MODEL), _const_spec(n1.shape), _const_spec(wg.shape), _const_spec(wu.shape),
                  _const_spec(wd.shape), _const_spec(n2.shape), _const_spec(win.shape)],
        out_specs=[row_spec(D_MODEL), row_spec(D_IN)],
        out_shape=[jax.ShapeDtypeStruct((rows, D_MODEL), F32), jax.ShapeDtypeStruct((rows, D_IN), F32)],
        compiler_params=_params(1),
        name="ffn_in",
    )(x, n1, wg, wu, wd, n2, win)


def _ffn_out_kernel(x_ref, n1_ref, wg_ref, wu_ref, wd_ref, nf_ref, y_ref):
    x = x_ref[...]
    h = _rms(x, n1_ref[...]).astype(BF16)
    x4 = x + 0.5 * _swiglu(h, wg_ref, wu_ref, wd_ref)
    y_ref[...] = _rms(x4, nf_ref[...])


def _ffn_out(x, n1, wg, wu, wd, nf, tm):
    rows = x.shape[0]
    row_spec = pl.BlockSpec((tm, D_MODEL), lambda i: (i, 0))
    return pl.pallas_call(
        _ffn_out_kernel,
        grid=(rows // tm,),
        in_specs=[row_spec, _const_spec(n1.shape), _const_spec(wg.shape), _const_spec(wu.shape),
                  _const_spec(wd.shape), _const_spec(nf.shape)],
        out_specs=row_spec,
        out_shape=jax.ShapeDtypeStruct((rows, D_MODEL), F32),
        compiler_params=_params(1),
        name="ffn_out",
    )(x, n1, wg, wu, wd, nf)


def _mem_kv_kernel(m_ref, wk_ref, wv_ref, k_ref, v_ref, kb_ref, vb_ref):
    m = m_ref[0].astype(BF16)
    for w_ref, o_ref, ob_ref in ((wk_ref, k_ref, kb_ref), (wv_ref, v_ref, vb_ref)):
        kv = _dot(m, w_ref[...])
        ob_ref[0] = kv.astype(BF16)
        for h in range(XATTN_HEADS):
            o_ref[:, h, :] = kv[:, h * XATTN_HEAD_DIM:(h + 1) * XATTN_HEAD_DIM]


def _mem_kv(mem, wk, wv):
    bsz = mem.shape[0]
    row_spec = pl.BlockSpec((1, N_MEM, D_MODEL), lambda b: (b, 0, 0))
    cache_spec = pl.BlockSpec((None, None, N_MEM, XATTN_HEADS, XATTN_HEAD_DIM), lambda b: (0, b, 0, 0, 0))
    cache_shape = jax.ShapeDtypeStruct((1, bsz, N_MEM, XATTN_HEADS, XATTN_HEAD_DIM), F32)
    return pl.pallas_call(
        _mem_kv_kernel,
        grid=(bsz,),
        in_specs=[row_spec, _const_spec(wk.shape), _const_spec(wv.shape)],
        out_specs=[cache_spec, cache_spec, row_spec, row_spec],
        out_shape=[cache_shape, cache_shape] + [jax.ShapeDtypeStruct((bsz, N_MEM, D_MODEL), BF16)] * 2,
        compiler_params=_params(1),
        name="mem_kv",
    )(mem, wk, wv)


def _rotary(x, cos2, sin2):
    return x * cos2 + pltpu.roll(x, RET_HEAD_DIM // 2, axis=1) * sin2


def _group_norm_gate(o, gain, g):
    mu = jnp.mean(o, axis=-1, keepdims=True)
    oc = o - mu
    var = jnp.mean(oc * oc, axis=-1, keepdims=True)
    on = oc * lax.rsqrt(var + EPS) * gain
    return (g * jax.nn.sigmoid(g)) * on


def _softplus(x):
    return jnp.maximum(x, 0.0) + jnp.log1p(jnp.exp(-jnp.abs(x)))


def _lru_group(u3, hist3, gi, cw_ref, cb_ref, wgate_ref, ba_ref, bx_ref, lam_ref):
    cols = slice(gi * LRU_GROUP, (gi + 1) * LRU_GROUP)
    uc3 = _conv_taps(u3, hist3, cw_ref[:, cols], cb_ref[:, cols])
    uc = uc3.reshape(u3.shape[0] * SUBLANES, LRU_GROUP)
    pre = _dot(uc.astype(BF16), wgate_ref[gi])
    r = jax.nn.sigmoid(pre[:, :LRU_GROUP] + ba_ref[:, cols])
    i = jax.nn.sigmoid(pre[:, LRU_GROUP:] + bx_ref[:, cols])
    log_a = (-LRU_C) * r * _softplus(-lam_ref[:, cols])
    a = jnp.exp(log_a)
    gx = jnp.sqrt(1.0 - a * a) * (i * uc)
    return a.reshape(u3.shape), gx.reshape(u3.shape)


def _slab_scan(a, g):
    t = lax.broadcasted_iota(jnp.int32, a.shape, 1)
    d = 1
    while d < SUBLANES:
        keep = t >= d
        g_prev = jnp.where(keep, pltpu.roll(g, d, axis=1), 0.0)
        a_prev = jnp.where(keep, pltpu.roll(a, d, axis=1), 1.0)
        g = g + a * g_prev
        a = a * a_prev
        d *= 2
    return a, g


def _gelu_tanh(x):
    return 0.5 * x * (1.0 + jnp.tanh(math.sqrt(2.0 / math.pi) * (x + 0.044715 * (x * x * x))))


def _conv_taps(u3, hist3, cw, cb):
    t = lax.broadcasted_iota(jnp.int32, u3.shape, 1)
    acc = u3 * cw[CONV_WIDTH - 1:CONV_WIDTH, :] + cb
    for k in range(1, CONV_WIDTH):
        prev = jnp.where(t >= k, pltpu.roll(u3, k, axis=1), pltpu.roll(hist3, k, axis=1))
        acc = acc + prev * cw[CONV_WIDTH - 1 - k:CONV_WIDTH - k, :]
    return acc


def _attention_head(q_bf, k_bf, v_bf):
    sc = lax.dot_general(q_bf, k_bf, (((1,), (1,)), ((), ())), preferred_element_type=F32)
    p = jnp.exp(sc - jnp.max(sc, axis=-1, keepdims=True))
    l = jnp.sum(p, axis=-1, keepdims=True)
    return (_dot(p.astype(BF16), v_bf) / l).astype(BF16)


def _interleave(a, b):
    out, taken = [], 0
    for i, unit in enumerate(a):
        out.append(unit)
        want = (i + 1) * len(b) // len(a)
        out.extend(b[taken:want])
        taken = want
    return out


def _mixer_ffn_prompt_kernel(z_ref, x1_ref, cs_ref, sn_ref, mk_ref, mv_ref, dmask_ref, qdec_ref, kdec_ref,
                             gn_ref, cw_ref, cb_ref, wgate_ref, ba_ref, bx_ref, lam_ref, ln_ref,
                             wout_ref, xn_ref, wq_ref, wo_ref, n2_ref, wg_ref, wu_ref, wd_ref, nf_ref,
                             y_ref, so_ref, ho_ref, cv_ref,
                             hist_ref, s_ref, h_ref, x3_ref,
                             *, tt, tiles_per_seq, n_tiles, chunk_dec):
    step = pl.program_id(0)
    t_idx = jnp.minimum(step, n_tiles - 1) % tiles_per_seq

    @pl.when(step == 0)
    def _():
        x3_ref[...] = jnp.zeros_like(x3_ref)

    @pl.when(t_idx == 0)
    def _():
        s_ref[...] = jnp.zeros_like(s_ref)
        h_ref[...] = jnp.zeros_like(h_ref)
        hist_ref[...] = jnp.zeros_like(hist_ref)

    x3_prev = x3_ref[...]
    hf = _rms(x3_prev, n2_ref[...]).astype(BF16)
    ffn = {"act": [], "x4": []}

    def ffn_up_chunk(j):
        cols = slice(j * FFN_CHUNK, (j + 1) * FFN_CHUNK)
        g = _dot(hf, wg_ref[:, cols])
        u = _dot(hf, wu_ref[:, cols])
        ffn["act"].append(((g * jax.nn.sigmoid(g)) * u).astype(BF16))

    def ffn_down_chunk(j):
        if j == 0:
            ffn["act"] = jnp.concatenate(ffn["act"], axis=-1)
        cols = slice(j * FFN_CHUNK, (j + 1) * FFN_CHUNK)
        ffn["x4"].append(x3_prev[:, cols] + 0.5 * _dot(ffn["act"], wd_ref[:, cols]))

    def ffn_finish():
        y_ref[0] = _rms(jnp.concatenate(ffn["x4"], axis=-1), nf_ref[...])

    ffn_units = [functools.partial(ffn_up_chunk, j) for j in range(D_FF // FFN_CHUNK)]
    ffn_units += [functools.partial(ffn_down_chunk, j) for j in range(D_MODEL // FFN_CHUNK)] + [ffn_finish]

    mx = {"ret": [[] for _ in range(tt // RET_CHUNK)], "hs": [], "hs_sq": [], "lru": []}

    def retention_unit(c, h):
        rows = slice(c * RET_CHUNK, (c + 1) * RET_CHUNK)
        cos2 = cs_ref[rows, :]
        sin2 = sn_ref[rows, :]
        col = lambda j: slice(j * D_RET + h * RET_HEAD_DIM, j * D_RET + (h + 1) * RET_HEAD_DIM)
        q = _rotary(z_ref[0, rows, col(0)], cos2, sin2)
        k = _rotary(z_ref[0, rows, col(1)], cos2, sin2) * (RET_HEAD_DIM ** -0.5)
        v_bf = z_ref[0, rows, col(2)].astype(BF16)
        g = z_ref[0, rows, col(3)]
        q_bf = q.astype(BF16)
        sc = lax.dot_general(q_bf, k.astype(BF16), (((1,), (1,)), ((), ())), preferred_element_type=F32)
        sc = sc * dmask_ref[h]
        s_old = s_ref[h]
        o = _dot(sc.astype(BF16), v_bf) + _dot(q_bf, s_old.astype(BF16)) * qdec_ref[h]
        kd_bf = (k * kdec_ref[h]).astype(BF16)
        s_ref[h] = chunk_dec[h] * s_old + lax.dot_general(
            kd_bf, v_bf, (((0,), (0,)), ((), ())), preferred_element_type=F32)
        gain = gn_ref[:, h * RET_HEAD_DIM:(h + 1) * RET_HEAD_DIM]
        mx["ret"][c].append(_group_norm_gate(o, gain, g).astype(BF16))

    n_slab = tt // SUBLANES

    def lru_group_unit(gi):
        cols = slice(gi * LRU_GROUP, (gi + 1) * LRU_GROUP)
        u3 = z_ref[0, :, 4 * D_RET + gi * LRU_GROUP:4 * D_RET + (gi + 1) * LRU_GROUP].reshape(
            n_slab, SUBLANES, LRU_GROUP)
        hist3 = jnp.concatenate([hist_ref[:, cols].reshape(1, SUBLANES, LRU_GROUP), u3[:-1]], axis=0)
        hist_ref[:, cols] = u3[n_slab - 1]
        a3, gx3 = _lru_group(u3, hist3, gi, cw_ref, cb_ref, wgate_ref, ba_ref, bx_ref, lam_ref)
        a_cum, h_loc = _slab_scan(a3, gx3)
        carry = h_ref[:, cols]
        slabs = []
        for s in range(n_slab):
            hs_s = h_loc[s] + a_cum[s] * carry
            slabs.append(hs_s)
            carry = hs_s[SUBLANES - 1:, :]
        h_ref[:, cols] = carry
        hs = jnp.concatenate(slabs, axis=0)
        mx["hs"].append(hs)
        mx["hs_sq"].append(jnp.sum(hs * hs, axis=-1, keepdims=True))

    def lru_norm_unit():
        inv = lax.rsqrt(sum(mx["hs_sq"]) * (1.0 / D_LRU) + EPS)
        for gi, hs in enumerate(mx["hs"]):
            cols = slice(gi * LRU_GROUP, (gi + 1) * LRU_GROUP)
            gate = z_ref[0, :, 4 * D_RET + D_LRU + gi * LRU_GROUP:4 * D_RET + D_LRU + (gi + 1) * LRU_GROUP]
            mx["lru"].append((hs * inv * ln_ref[:, cols] * _gelu_tanh(gate)).astype(BF16))

    def out_proj_unit():
        ret = jnp.concatenate([jnp.concatenate(heads, axis=-1) for heads in mx["ret"]], axis=0)
        mix = jnp.concatenate([ret] + mx["lru"], axis=-1)
        x2 = x1_ref[0] + _dot(mix, wout_ref[...])
        hx = _rms(x2, xn_ref[...]).astype(BF16)
        mx["x2"] = x2
        mx["q"] = (_dot(hx, wq_ref[...]) * (XATTN_HEAD_DIM ** -0.5)).astype(BF16)
        mx["o"] = []

    def attention_unit(h):
        sl = slice(h * XATTN_HEAD_DIM, (h + 1) * XATTN_HEAD_DIM)
        mx["o"].append(_attention_head(mx["q"][:, sl], mk_ref[0, :, sl], mv_ref[0, :, sl]))

    def attn_out_unit():
        x3_ref[...] = mx["x2"] + _dot(jnp.concatenate(mx["o"], axis=-1), wo_ref[...])

    mixer_units = [functools.partial(retention_unit, c, h)
                   for c in range(tt // RET_CHUNK) for h in range(RET_HEADS)]
    mixer_units += [functools.partial(lru_group_unit, gi) for gi in range(D_LRU // LRU_GROUP)]
    mixer_units += [lru_norm_unit, out_proj_unit]
    mixer_units += [functools.partial(attention_unit, h) for h in range(XATTN_HEADS)]

    for unit in _interleave(mixer_units, ffn_units):
        unit()
    attn_out_unit()

    @pl.when(jnp.logical_and(t_idx == tiles_per_seq - 1, step < n_tiles))
    def _():
        so_ref[0] = s_ref[...]
        ho_ref[0] = h_ref[...]
        cv_ref[0] = hist_ref[SUBLANES - (CONV_WIDTH - 1):, :]


def _mixer_ffn_prompt(z, x1, cos2, sin2, mk, mv, dmask, qdec, kdec, chunk_dec, small, wout, xn, wq, wo,
                      n2, wg, wu, wd, nf, tt):
    bsz, seq, _ = z.shape
    tps = seq // tt
    n_tiles = bsz * tps
    consts = [dmask, qdec, kdec, *small, wout, xn, wq, wo, n2, wg, wu, wd, nf]
    cur = lambda s: jnp.minimum(s, n_tiles - 1)
    prev = lambda s: jnp.maximum(s - 1, 0)
    seq_spec = lambda w: pl.BlockSpec((1, tt, w), lambda s: (cur(s) // tps, cur(s) % tps, 0))
    tab_spec = pl.BlockSpec((tt, RET_HEAD_DIM), lambda s: (cur(s) % tps, 0))
    mem_spec = pl.BlockSpec((1, N_MEM, D_MODEL), lambda s: (cur(s) // tps, 0, 0))
    state_spec = lambda *tail: pl.BlockSpec((1, *tail), lambda s: (cur(s) // tps,) + (0,) * len(tail))
    return pl.pallas_call(
        functools.partial(_mixer_ffn_prompt_kernel, tt=tt, tiles_per_seq=tps, n_tiles=n_tiles,
                          chunk_dec=chunk_dec),
        grid=(n_tiles + 1,),
        in_specs=[seq_spec(D_IN), seq_spec(D_MODEL), tab_spec, tab_spec, mem_spec, mem_spec]
        + [_const_spec(c.shape) for c in consts],
        out_specs=[pl.BlockSpec((1, tt, D_MODEL), lambda s: (prev(s) // tps, prev(s) % tps, 0)),
                   state_spec(RET_HEADS, RET_HEAD_DIM, RET_HEAD_DIM), state_spec(1, D_LRU),
                   state_spec(CONV_WIDTH - 1, D_LRU)],
        out_shape=[jax.ShapeDtypeStruct((bsz, seq, D_MODEL), F32),
                   jax.ShapeDtypeStruct((bsz, RET_HEADS, RET_HEAD_DIM, RET_HEAD_DIM), F32),
                   jax.ShapeDtypeStruct((bsz, 1, D_LRU), F32),
                   jax.ShapeDtypeStruct((bsz, CONV_WIDTH - 1, D_LRU), F32)],
        scratch_shapes=[pltpu.VMEM((SUBLANES, D_LRU), F32),
                        pltpu.VMEM((RET_HEADS, RET_HEAD_DIM, RET_HEAD_DIM), F32), pltpu.VMEM((1, D_LRU), F32),
                        pltpu.VMEM((tt, D_MODEL), F32)],
        compiler_params=_params(1),
        name="mixer_ffn_prompt",
    )(z, x1, cos2, sin2, mk, mv, *consts)


def _mixer_sample_kernel(z_ref, x1_ref, cs_ref, sn_ref, s0_ref, h0_ref, c0_ref, dmask_ref, qdec_ref, kdec_ref,
                         gn_ref, cw_ref, cb_ref, wgate_ref, ba_ref, bx_ref, lam_ref, ln_ref,
                         wout_ref, xn_ref, wq_ref,
                         x2_ref, q_ref, s_ref, h_ref, cv_ref,
                         mix_ref, qs_ref, ks_ref, kd_ref, vs_ref, cross_ref, hist_ref, *, g_seq, chunk_dec):
    rows_n = g_seq * SUBLANES
    cos2 = cs_ref[...]
    sin2 = sn_ref[...]
    for h in range(RET_HEADS):
        col = lambda j: slice(j * D_RET + h * RET_HEAD_DIM, j * D_RET + (h + 1) * RET_HEAD_DIM)
        q = _rotary(z_ref[:, col(0)], cos2, sin2)
        k = _rotary(z_ref[:, col(1)], cos2, sin2) * (RET_HEAD_DIM ** -0.5)
        qs_ref[h] = q
        ks_ref[h] = k
        kd_ref[h] = k * kdec_ref[h]
        vs_ref[h] = z_ref[:, col(2)]

    def seq_body(s, carry):
        r0 = pl.multiple_of(s * SUBLANES, SUBLANES)
        for h in range(RET_HEADS):
            s_old = s0_ref[s, h]
            q_bf = qs_ref[h, pl.ds(r0, SUBLANES), :].astype(BF16)
            cross_ref[h, pl.ds(r0, SUBLANES), :] = _dot(q_bf, s_old.astype(BF16))
            kd_bf = kd_ref[h, pl.ds(r0, SUBLANES), :].astype(BF16)
            v_bf = vs_ref[h, pl.ds(r0, SUBLANES), :].astype(BF16)
            s_ref[s, h] = chunk_dec[h] * s_old + lax.dot_general(
                kd_bf, v_bf, (((0,), (0,)), ((), ())), preferred_element_type=F32)
        return carry

    lax.fori_loop(0, g_seq, seq_body, 0)

    for h in range(RET_HEADS):
        col_g = slice(3 * D_RET + h * RET_HEAD_DIM, 3 * D_RET + (h + 1) * RET_HEAD_DIM)
        q_bf = qs_ref[h].astype(BF16)
        k_bf = ks_ref[h].astype(BF16)
        sc = lax.dot_general(q_bf, k_bf, (((1,), (1,)), ((), ())), preferred_element_type=F32) * dmask_ref[h]
        o = _dot(sc.astype(BF16), vs_ref[h].astype(BF16)) + cross_ref[h] * qdec_ref[h]
        gain = gn_ref[:, h * RET_HEAD_DIM:(h + 1) * RET_HEAD_DIM]
        mix_ref[:, h * RET_HEAD_DIM:(h + 1) * RET_HEAD_DIM] = _group_norm_gate(o, gain, z_ref[:, col_g]).astype(BF16)

    hist_ref[...] = jnp.zeros_like(hist_ref)
    hist_ref[:, SUBLANES - (CONV_WIDTH - 1):, :] = c0_ref[...]
    hs_groups = []
    for gi in range(D_LRU // LRU_GROUP):
        cols = slice(gi * LRU_GROUP, (gi + 1) * LRU_GROUP)
        u3 = z_ref[:, 4 * D_RET + gi * LRU_GROUP:4 * D_RET + (gi + 1) * LRU_GROUP].reshape(
            g_seq, SUBLANES, LRU_GROUP)
        cv_ref[:, :, cols] = u3[:, SUBLANES - (CONV_WIDTH - 1):, :]
        a3, gx3 = _lru_group(u3, hist_ref[:, :, cols], gi, cw_ref, cb_ref, wgate_ref, ba_ref, bx_ref, lam_ref)
        a_cum, h_loc = _slab_scan(a3, gx3)
        hs3 = h_loc + a_cum * h0_ref[:, :, cols]
        h_ref[:, :, cols] = hs3[:, SUBLANES - 1:, :]
        hs_groups.append(hs3.reshape(rows_n, LRU_GROUP))
    hs = jnp.concatenate(hs_groups, axis=-1)
    gate = z_ref[:, 4 * D_RET + D_LRU:]
    mix_ref[:, D_RET:] = (_rms(hs, ln_ref[...]) * _gelu_tanh(gate)).astype(BF16)

    x2 = x1_ref[...] + _dot(mix_ref[...], wout_ref[...])
    x2_ref[...] = x2
    hx = _rms(x2, xn_ref[...]).astype(BF16)
    q = (_dot(hx, wq_ref[...]) * (XATTN_HEAD_DIM ** -0.5)).reshape(g_seq, SUBLANES, D_MODEL)
    q_heads = [q[:, :, h * XATTN_HEAD_DIM:(h + 1) * XATTN_HEAD_DIM] for h in range(XATTN_HEADS)]
    q_ref[...] = jnp.concatenate(q_heads, axis=1).reshape(rows_n * XATTN_HEADS, XATTN_HEAD_DIM).astype(BF16)


def _mixer_sample(z, x1, cos2, sin2, s0, h0, c0, dmask, qdec, kdec, chunk_dec, small, wout, xn, wq, g_seq):
    rows = z.shape[0]
    n_seq = rows // SUBLANES
    rn = g_seq * SUBLANES
    consts = [dmask, qdec, kdec, *small, wout, xn, wq]
    row_spec = lambda w: pl.BlockSpec((rn, w), lambda i: (i, 0))
    st_spec = pl.BlockSpec((g_seq, RET_HEADS, RET_HEAD_DIM, RET_HEAD_DIM), lambda i: (i, 0, 0, 0))
    h_spec = pl.BlockSpec((g_seq, 1, D_LRU), lambda i: (i, 0, 0))
    c_spec = pl.BlockSpec((g_seq, CONV_WIDTH - 1, D_LRU), lambda i: (i, 0, 0))
    head_buf = pltpu.VMEM((RET_HEADS, rn, RET_HEAD_DIM), F32)
    return pl.pallas_call(
        functools.partial(_mixer_sample_kernel, g_seq=g_seq, chunk_dec=chunk_dec),
        grid=(n_seq // g_seq,),
        in_specs=[row_spec(D_IN), row_spec(D_MODEL), _const_spec(cos2.shape), _const_spec(sin2.shape),
                  st_spec, h_spec, c_spec] + [_const_spec(c.shape) for c in consts],
        out_specs=[row_spec(D_MODEL), pl.BlockSpec((rn * XATTN_HEADS, XATTN_HEAD_DIM), lambda i: (i, 0)),
                   st_spec, h_spec, c_spec],
        out_shape=[jax.ShapeDtypeStruct((rows, D_MODEL), F32),
                   jax.ShapeDtypeStruct((rows * XATTN_HEADS, XATTN_HEAD_DIM), BF16),
                   jax.ShapeDtypeStruct(s0.shape, F32), jax.ShapeDtypeStruct(h0.shape, F32),
                   jax.ShapeDtypeStruct(c0.shape, F32)],
        scratch_shapes=[pltpu.VMEM((rn, D_MODEL), BF16), head_buf, head_buf, head_buf, head_buf, head_buf,
                        pltpu.VMEM((g_seq, SUBLANES, D_LRU), F32)],
        compiler_params=_params(1),
        name="mixer_sample",
    )(z, x1, cos2, sin2, s0, h0, c0, *consts)


def _xattn_sample_kernel(q_ref, x2_ref, k_ref, v_ref, wo_ref, x3_ref, o_ref, *, g_seq):
    n_q = XATTN_HEADS * SUBLANES
    n_kv = N_MEM * XATTN_HEADS
    col_head = lax.broadcasted_iota(jnp.int32, (n_q, n_kv), 1) % XATTN_HEADS
    row_head = lax.broadcasted_iota(jnp.int32, (n_q, n_kv), 0) // SUBLANES
    own_head = col_head == row_head
    for s in range(g_seq):
        k_all = k_ref[s].reshape(n_kv, XATTN_HEAD_DIM).astype(BF16)
        v_all = v_ref[s].reshape(n_kv, XATTN_HEAD_DIM).astype(BF16)
        sc = lax.dot_general(q_ref[s * n_q:(s + 1) * n_q, :], k_all, (((1,), (1,)), ((), ())),
                             preferred_element_type=F32)
        sc = jnp.where(own_head, sc, -jnp.inf)
        p = jnp.exp(sc - jnp.max(sc, axis=-1, keepdims=True))
        o = _dot(p.astype(BF16), v_all) / jnp.sum(p, axis=-1, keepdims=True)
        for h in range(XATTN_HEADS):
            o_ref[s * SUBLANES:(s + 1) * SUBLANES, h * XATTN_HEAD_DIM:(h + 1) * XATTN_HEAD_DIM] = (
                o[h * SUBLANES:(h + 1) * SUBLANES, :])
    x3_ref[...] = x2_ref[...] + _dot(o_ref[...].astype(BF16), wo_ref[...])


def _xattn_sample(q, x2, ck, cv, wo, g_seq):
    rows = x2.shape[0]
    rn = g_seq * SUBLANES
    row_spec = pl.BlockSpec((rn, D_MODEL), lambda i: (i, 0))
    q_spec = pl.BlockSpec((rn * XATTN_HEADS, XATTN_HEAD_DIM), lambda i: (i, 0))
    mem_spec = pl.BlockSpec((None, g_seq, N_MEM, XATTN_HEADS, XATTN_HEAD_DIM), lambda i: (0, i, 0, 0, 0))
    return pl.pallas_call(
        functools.partial(_xattn_sample_kernel, g_seq=g_seq),
        grid=(rows // rn,),
        in_specs=[q_spec, row_spec, mem_spec, mem_spec, _const_spec(wo.shape)],
        out_specs=row_spec,
        out_shape=jax.ShapeDtypeStruct((rows, D_MODEL), F32),
        scratch_shapes=[pltpu.VMEM((rn, D_MODEL), F32)],
        compiler_params=_params(1),
        name="xattn_sample",
    )(q, x2, ck, cv, wo)


def _rope_tables(pos):
    half = RET_HEAD_DIM // 2
    inv = ROPE_BASE ** (-jnp.arange(half, dtype=F32) / half)
    ang = pos.astype(F32)[:, None] * inv[None, :]
    cos, sin = jnp.cos(ang), jnp.sin(ang)
    return jnp.concatenate([cos, cos], axis=-1), jnp.concatenate([-sin, sin], axis=-1)


def _decay_tables(c, n_rep):
    lg = jnp.log(1.0 - 2.0 ** (-5.0 - jnp.arange(RET_HEADS, dtype=F32)))
    idx = jnp.arange(c, dtype=F32)
    diff = idx[:, None] - idx[None, :]
    dmask = jnp.where(diff[None] >= 0, jnp.exp(lg[:, None, None] * jnp.maximum(diff, 0.0)[None]), 0.0)
    q_dec = jnp.exp(lg[:, None] * (idx[None, :] + 1.0))
    k_dec = jnp.exp(lg[:, None] * (c - 1.0 - idx[None, :]))
    n = c * n_rep
    if n_rep > 1:
        eye = jnp.eye(n_rep, dtype=F32)
        dmask = jnp.einsum('ab,hij->haibj', eye, dmask).reshape(RET_HEADS, n, n)
        q_dec = jnp.tile(q_dec, (1, n_rep))
        k_dec = jnp.tile(k_dec, (1, n_rep))
    bcast = lambda d: jnp.broadcast_to(d[:, :, None], (RET_HEADS, n, RET_HEAD_DIM))
    gammas = [1.0 - 2.0 ** (-5.0 - h) for h in range(RET_HEADS)]
    chunk_dec = tuple(g ** c for g in gammas)
    return dmask, bcast(q_dec), bcast(k_dec), chunk_dec


def _gate_weights(wa, wx):
    per = LRU_GROUP // LRU_BLOCK_DIM
    eye = jnp.eye(per, dtype=wa.dtype)

    def grouped(w):
        w = w.reshape(LRU_BLOCKS // per, per, LRU_BLOCK_DIM, LRU_BLOCK_DIM)
        return jnp.einsum('ab,gakj->gakbj', eye, w).reshape(LRU_BLOCKS // per, LRU_GROUP, LRU_GROUP)

    return jnp.concatenate([grouped(wa), grouped(wx)], axis=-1).astype(BF16)


def kernel(x_prompt, x_sample, state_ret, state_lru_h, state_lru_conv, cache_mem_k, cache_mem_v, mem_prompt,
           ffn1_norm, ffn1_wg, ffn1_wu, ffn1_wd, mix_norm, w_in, ret_gn_gain, conv_w, conv_b,
           lru_wa, lru_ba, lru_wx, lru_bx, lru_lambda, lru_norm, w_out,
           xattn_norm, xattn_wq, xattn_wk, xattn_wv, xattn_wo,
           ffn2_norm, ffn2_wg, ffn2_wu, ffn2_wd, final_norm):
    depth = ffn1_wg.shape[0]
    assert depth == 1, "single-layer trunk"
    bp, tp, _ = x_prompt.shape
    bs, ts, _ = x_sample.shape
    assert ts == SUBLANES and tp % RET_CHUNK == 0
    bf = lambda w: w[0].astype(BF16)
    row = lambda v: v.reshape(1, -1)

    f1 = (row(ffn1_norm[0]), bf(ffn1_wg), bf(ffn1_wu), bf(ffn1_wd), row(mix_norm[0]), bf(w_in))
    f2 = (row(ffn2_norm[0]), bf(ffn2_wg), bf(ffn2_wu), bf(ffn2_wd), row(final_norm))
    small = [row(ret_gn_gain[0]), conv_w[0], row(conv_b[0]), _gate_weights(lru_wa[0], lru_wx[0]),
             row(lru_ba[0]), row(lru_bx[0]), row(lru_lambda[0]), row(lru_norm[0])]
    wout, xn, wq, wo = bf(w_out), row(xattn_norm[0]), bf(xattn_wq), bf(xattn_wo)

    tm = 256
    mk, mv, mk_bf, mv_bf = _mem_kv(mem_prompt, bf(xattn_wk), bf(xattn_wv))
    x1p, zp = _ffn_in(x_prompt.reshape(bp * tp, D_MODEL), *f1, tm)
    cos_p, sin_p = _rope_tables(jnp.arange(tp, dtype=jnp.int32))
    dmask, qdec, kdec, cdec = _decay_tables(RET_CHUNK, 1)
    tt = 256
    y_p, s_p, h_p, c_p = _mixer_ffn_prompt(
        zp.reshape(bp, tp, D_IN), x1p.reshape(bp, tp, D_MODEL), cos_p, sin_p,
        mk_bf, mv_bf, dmask, qdec, kdec, cdec, small, wout, xn, wq, wo, *f2, tt)

    g_seq = 16
    x1s, zs = _ffn_in(x_sample.reshape(bs * ts, D_MODEL), *f1, tm)
    cos_s, sin_s = _rope_tables(PAST_LEN + jnp.arange(ts, dtype=jnp.int32))
    cos_s, sin_s = jnp.tile(cos_s, (g_seq, 1)), jnp.tile(sin_s, (g_seq, 1))
    dmask8, qdec8, kdec8, cdec8 = _decay_tables(ts, g_seq)
    x2s, qs, s_s, h_s, c_s = _mixer_sample(
        zs, x1s, cos_s, sin_s, state_ret[0], state_lru_h[0].reshape(bs, 1, D_LRU), state_lru_conv[0],
        dmask8, qdec8, kdec8, cdec8, small, wout, xn, wq, g_seq)
    g_att = 8
    x3s = _xattn_sample(qs, x2s, cache_mem_k, cache_mem_v, wo, g_att)
    y_s = _ffn_out(x3s, *f2, tm).reshape(bs, ts, D_MODEL)

    return (y_p, y_s, s_p[None], h_p.reshape(1, bp, D_LRU), c_p[None],
            mk, mv, s_s[None], h_s.reshape(1, bs, D_LRU), c_s[None])
```

```python
import functools
import math

import jax
import jax.numpy as jnp
from jax import lax
from jax.experimental import pallas as pl
from jax.experimental.pallas import tpu as pltpu

D_MODEL = 1024
PAST_LEN = 16384
N_MEM = 256
D_RET = 512
D_LRU = 512
RET_HEADS = 4
RET_HEAD_DIM = 128
RET_CHUNK = 128
ROPE_BASE = 10000.0
LRU_BLOCKS = 8
LRU_BLOCK_DIM = 64
CONV_WIDTH = 4
LRU_C = 8.0
XATTN_HEADS = 4
XATTN_HEAD_DIM = 256
D_FF = 2816
D_IN = 3072
EPS = 1e-6

SUBLANES = 8
LANES = 128
FFN_CHUNK = 256
LRU_GROUP = 128
VMEM_LIMIT_BYTES = 56 * 1024 * 1024

F32 = jnp.float32
BF16 = jnp.bfloat16


def _rms(x, g):
    ms = jnp.mean(x * x, axis=-1, keepdims=True)
    return x * lax.rsqrt(ms + EPS) * g


def _dot(a, b):
    return jnp.dot(a, b, preferred_element_type=F32)


def _swiglu(h_bf, wg_ref, wu_ref, wd_ref):
    g = _dot(h_bf, wg_ref[...])
    u = _dot(h_bf, wu_ref[...])
    a = (g * jax.nn.sigmoid(g)) * u
    return _dot(a.astype(BF16), wd_ref[...])


def _const_spec(shape):
    nd = len(shape)
    return pl.BlockSpec(shape, lambda *_: (0,) * nd)


def _params(n_axes):
    return pltpu.CompilerParams(dimension_semantics=("arbitrary",) * n_axes,
                                vmem_limit_bytes=VMEM_LIMIT_BYTES)


def _ffn_in_kernel(x_ref, n1_ref, wg_ref, wu_ref, wd_ref, n2_ref, win_ref, x1_ref, z_ref):
    x = x_ref[...]
    h = _rms(x, n1_ref[...]).astype(BF16)
    x1 = x + 0.5 * _swiglu(h, wg_ref, wu_ref, wd_ref)
    x1_ref[...] = x1
    hn = _rms(x1, n2_ref[...]).astype(BF16)
    z_ref[...] = _dot(hn, win_ref[...])


def _ffn_in(x, n1, wg, wu, wd, n2, win, tm):
    rows = x.shape[0]
    row_spec = lambda w: pl.BlockSpec((tm, w), lambda i: (i, 0))
    return pl.pallas_call(
        _ffn_in_kernel,
        grid=(rows // tm,),
        in_specs=[row_spec(D_MODEL), _const_spec(n1.shape), _const_spec(wg.shape), _const_spec(wu.shape),
                  _const_spec(wd.shape), _const_spec(n2.shape), _const_spec(win.shape)],
        out_specs=[row_spec(D_MODEL), row_spec(D_IN)],
        out_shape=[jax.ShapeDtypeStruct((rows, D_MODEL), F32), jax.ShapeDtypeStruct((rows, D_IN), F32)],
        compiler_params=_params(1),
        name="ffn_in",
    )(x, n1, wg, wu, wd, n2, win)


def _ffn_out_kernel(x_ref, n1_ref, wg_ref, wu_ref, wd_ref, nf_ref, y_ref):
    x = x_ref[...]
    h = _rms(x, n1_ref[...]).astype(BF16)
    x4 = x + 0.5 * _swiglu(h, wg_ref, wu_ref, wd_ref)
    y_ref[...] = _rms(x4, nf_ref[...])


def _ffn_out(x, n1, wg, wu, wd, nf, tm):
    rows = x.shape[0]
    row_spec = pl.BlockSpec((tm, D_MODEL), lambda i: (i, 0))
    return pl.pallas_call(
        _ffn_out_kernel,
        grid=(rows // tm,),
        in_specs=[row_spec, _const_spec(n1.shape), _const_spec(wg.shape), _const_spec(wu.shape),
                  _const_spec(wd.shape), _const_spec(nf.shape)],
        out_specs=row_spec,
        out_shape=jax.ShapeDtypeStruct((rows, D_MODEL), F32),
        compiler_params=_params(1),
        name="ffn_out",
    )(x, n1, wg, wu, wd, nf)


def _mem_kv_kernel(m_ref, wk_ref, wv_ref, k_ref, v_ref, kb_ref, vb_ref):
    m = m_ref[0].astype(BF16)
    for w_ref, o_ref, ob_ref in ((wk_ref, k_ref, kb_ref), (wv_ref, v_ref, vb_ref)):
        kv = _dot(m, w_ref[...])
        ob_ref[0] = kv.astype(BF16)
        for h in range(XATTN_HEADS):
            o_ref[:, h, :] = kv[:, h * XATTN_HEAD_DIM:(h + 1) * XATTN_HEAD_DIM]


def _mem_kv(mem, wk, wv):
    bsz = mem.shape[0]
    row_spec = pl.BlockSpec((1, N_MEM, D_MODEL), lambda b: (b, 0, 0))
    cache_spec = pl.BlockSpec((None, None, N_MEM, XATTN_HEADS, XATTN_HEAD_DIM), lambda b: (0, b, 0, 0, 0))
    cache_shape = jax.ShapeDtypeStruct((1, bsz, N_MEM, XATTN_HEADS, XATTN_HEAD_DIM), F32)
    return pl.pallas_call(
        _mem_kv_kernel,
        grid=(bsz,),
        in_specs=[row_spec, _const_spec(wk.shape), _const_spec(wv.shape)],
        out_specs=[cache_spec, cache_spec, row_spec, row_spec],
        out_shape=[cache_shape, cache_shape] + [jax.ShapeDtypeStruct((bsz, N_MEM, D_MODEL), BF16)] * 2,
        compiler_params=_params(1),
        name="mem_kv",
    )(mem, wk, wv)


def _rotary(x, cos2, sin2):
    return x * cos2 + pltpu.roll(x, RET_HEAD_DIM // 2, axis=1) * sin2


def _group_norm_gate(o, gain, g):
    mu = jnp.mean(o, axis=-1, keepdims=True)
    oc = o - mu
    var = jnp.mean(oc * oc, axis=-1, keepdims=True)
    on = oc * lax.rsqrt(var + EPS) * gain
    return (g * jax.nn.sigmoid(g)) * on


def _softplus(x):
    return jnp.maximum(x, 0.0) + jnp.log1p(jnp.exp(-jnp.abs(x)))


def _lru_group(u3, hist3, gi, cw_ref, cb_ref, wgate_ref, ba_ref, bx_ref, lam_ref):
    cols = slice(gi * LRU_GROUP, (gi + 1) * LRU_GROUP)
    uc3 = _conv_taps(u3, hist3, cw_ref[:, cols], cb_ref[:, cols])
    uc = uc3.reshape(u3.shape[0] * SUBLANES, LRU_GROUP)
    pre = _dot(uc.astype(BF16), wgate_ref[gi])
    r = jax.nn.sigmoid(pre[:, :LRU_GROUP] + ba_ref[:, cols])
    i = jax.nn.sigmoid(pre[:, LRU_GROUP:] + bx_ref[:, cols])
    log_a = (-LRU_C) * r * _softplus(-lam_ref[:, cols])
    a = jnp.exp(log_a)
    gx = jnp.sqrt(1.0 - a * a) * (i * uc)
    return a.reshape(u3.shape), gx.reshape(u3.shape)


def _slab_scan(a, g):
    t = lax.broadcasted_iota(jnp.int32, a.shape, 1)
    d = 1
    while d < SUBLANES:
        keep = t >= d
        g_prev = jnp.where(keep, pltpu.roll(g, d, axis=1), 0.0)
        a_prev = jnp.where(keep, pltpu.roll(a, d, axis=1), 1.0)
        g = g + a * g_prev
        a = a * a_prev
        d *= 2
    return a, g


def _gelu_tanh(x):
    return 0.5 * x * (1.0 + jnp.tanh(math.sqrt(2.0 / math.pi) * (x + 0.044715 * (x * x * x))))


def _conv_taps(u3, hist3, cw, cb):
    t = lax.broadcasted_iota(jnp.int32, u3.shape, 1)
    acc = u3 * cw[CONV_WIDTH - 1:CONV_WIDTH, :] + cb
    for k in range(1, CONV_WIDTH):
        prev = jnp.where(t >= k, pltpu.roll(u3, k, axis=1), pltpu.roll(hist3, k, axis=1))
        acc = acc + prev * cw[CONV_WIDTH - 1 - k:CONV_WIDTH - k, :]
    return acc


def _attention_head(q_bf, k_bf, v_bf):
    sc = lax.dot_general(q_bf, k_bf, (((1,), (1,)), ((), ())), preferred_element_type=F32)
    p = jnp.exp(sc - jnp.max(sc, axis=-1, keepdims=True))
    l = jnp.sum(p, axis=-1, keepdims=True)
    return (_dot(p.astype(BF16), v_bf) / l).astype(BF16)


def _interleave(a, b):
    out, taken = [], 0
    for i, unit in enumerate(a):
        out.append(unit)
        want = (i + 1) * len(b) // len(a)
        out.extend(b[taken:want])
        taken = want
    return out


def _mixer_ffn_prompt_kernel(z_ref, x1_ref, cs_ref, sn_ref, mk_ref, mv_ref, dmask_ref, qdec_ref, kdec_ref,
                             gn_ref, cw_ref, cb_ref, wgate_ref, ba_ref, bx_ref, lam_ref, ln_ref,
                             wout_ref, xn_ref, wq_ref, wo_ref, n2_ref, wg_ref, wu_ref, wd_ref, nf_ref,
                             y_ref, so_ref, ho_ref, cv_ref,
                             hist_ref, s_ref, h_ref, x3_ref,
                             *, tt, tiles_per_seq, n_tiles, chunk_dec):
    step = pl.program_id(0)
    t_idx = jnp.minimum(step, n_tiles - 1) % tiles_per_seq

    @pl.when(step == 0)
    def _():
        x3_ref[...] = jnp.zeros_like(x3_ref)

    @pl.when(t_idx == 0)
    def _():
        s_ref[...] = jnp.zeros_like(s_ref)
        h_ref[...] = jnp.zeros_like(h_ref)
        hist_ref[...] = jnp.zeros_like(hist_ref)

    x3_prev = x3_ref[...]
    hf = _rms(x3_prev, n2_ref[...]).astype(BF16)
    ffn = {"act": [], "x4": []}

    def ffn_up_chunk(j):
        cols = slice(j * FFN_CHUNK, (j + 1) * FFN_CHUNK)
        g = _dot(hf, wg_ref[:, cols])
        u = _dot(hf, wu_ref[:, cols])
        ffn["act"].append(((g * jax.nn.sigmoid(g)) * u).astype(BF16))

    def ffn_down_chunk(j):
        if j == 0:
            ffn["act"] = jnp.concatenate(ffn["act"], axis=-1)
        cols = slice(j * FFN_CHUNK, (j + 1) * FFN_CHUNK)
        ffn["x4"].append(x3_prev[:, cols] + 0.5 * _dot(ffn["act"], wd_ref[:, cols]))

    def ffn_finish():
        y_ref[0] = _rms(jnp.concatenate(ffn["x4"], axis=-1), nf_ref[...])

    ffn_units = [functools.partial(ffn_up_chunk, j) for j in range(D_FF // FFN_CHUNK)]
    ffn_units += [functools.partial(ffn_down_chunk, j) for j in range(D_MODEL // FFN_CHUNK)] + [ffn_finish]

    mx = {"ret": [], "hs": [], "hs_sq": [], "lru": []}

    def retention_unit(h):
        cos2 = cs_ref[...]
        sin2 = sn_ref[...]
        col = lambda j: slice(j * D_RET + h * RET_HEAD_DIM, j * D_RET + (h + 1) * RET_HEAD_DIM)
        q = _rotary(z_ref[0, :, col(0)], cos2, sin2)
        k = _rotary(z_ref[0, :, col(1)], cos2, sin2) * (RET_HEAD_DIM ** -0.5)
        v_bf = z_ref[0, :, col(2)].astype(BF16)
        g = z_ref[0, :, col(3)]
        q_bf = q.astype(BF16)
        sc = lax.dot_general(q_bf, k.astype(BF16), (((1,), (1,)), ((), ())), preferred_element_type=F32)
        sc = sc * dmask_ref[h]
        s_old = s_ref[h]
        o = _dot(sc.astype(BF16), v_bf) + _dot(q_bf, s_old.astype(BF16)) * qdec_ref[h]
        kd_bf = (k * kdec_ref[h]).astype(BF16)
        s_ref[h] = chunk_dec[h] * s_old + lax.dot_general(
            kd_bf, v_bf, (((0,), (0,)), ((), ())), preferred_element_type=F32)
        gain = gn_ref[:, h * RET_HEAD_DIM:(h + 1) * RET_HEAD_DIM]
        mx["ret"].append(_group_norm_gate(o, gain, g).astype(BF16))

    n_slab = tt // SUBLANES

    def lru_group_unit(gi):
        cols = slice(gi * LRU_GROUP, (gi + 1) * LRU_GROUP)
        u3 = z_ref[0, :, 4 * D_RET + gi * LRU_GROUP:4 * D_RET + (gi + 1) * LRU_GROUP].reshape(
            n_slab, SUBLANES, LRU_GROUP)
        hist3 = jnp.concatenate([hist_ref[:, cols].reshape(1, SUBLANES, LRU_GROUP), u3[:-1]], axis=0)
        hist_ref[:, cols] = u3[n_slab - 1]
        a3, gx3 = _lru_group(u3, hist3, gi, cw_ref, cb_ref, wgate_ref, ba_ref, bx_ref, lam_ref)
        a_cum, h_loc = _slab_scan(a3, gx3)
        carry = h_ref[:, cols]
        slabs = []
        for s in range(n_slab):
            hs_s = h_loc[s] + a_cum[s] * carry
            slabs.append(hs_s)
            carry = hs_s[SUBLANES - 1:, :]
        h_ref[:, cols] = carry
        hs = jnp.concatenate(slabs, axis=0)
        mx["hs"].append(hs)
        mx["hs_sq"].append(jnp.sum(hs * hs, axis=-1, keepdims=True))

    def lru_norm_unit():
        inv = lax.rsqrt(sum(mx["hs_sq"]) * (1.0 / D_LRU) + EPS)
        for gi, hs in enumerate(mx["hs"]):
            cols = slice(gi * LRU_GROUP, (gi + 1) * LRU_GROUP)
            gate = z_ref[0, :, 4 * D_RET + D_LRU + gi * LRU_GROUP:4 * D_RET + D_LRU + (gi + 1) * LRU_GROUP]
            mx["lru"].append((hs * inv * ln_ref[:, cols] * _gelu_tanh(gate)).astype(BF16))

    def out_proj_unit():
        mix = jnp.concatenate(mx["ret"] + mx["lru"], axis=-1)
        x2 = x1_ref[0] + _dot(mix, wout_ref[...])
        hx = _rms(x2, xn_ref[...]).astype(BF16)
        mx["x2"] = x2
        mx["q"] = (_dot(hx, wq_ref[...]) * (XATTN_HEAD_DIM ** -0.5)).astype(BF16)
        mx["o"] = []

    def attention_unit(h):
        sl = slice(h * XATTN_HEAD_DIM, (h + 1) * XATTN_HEAD_DIM)
        mx["o"].append(_attention_head(mx["q"][:, sl], mk_ref[0, :, sl], mv_ref[0, :, sl]))

    def attn_out_unit():
        x3_ref[...] = mx["x2"] + _dot(jnp.concatenate(mx["o"], axis=-1), wo_ref[...])

    mixer_units = [functools.partial(retention_unit, h) for h in range(RET_HEADS)]
    mixer_units += [functools.partial(lru_group_unit, gi) for gi in range(D_LRU // LRU_GROUP)]
    mixer_units += [lru_norm_unit, out_proj_unit]
    mixer_units += [functools.partial(attention_unit, h) for h in range(XATTN_HEADS)]

    for unit in _interleave(mixer_units, ffn_units):
        unit()
    attn_out_unit()

    @pl.when(jnp.logical_and(t_idx == tiles_per_seq - 1, step < n_tiles))
    def _():
        so_ref[0] = s_ref[...]
        ho_ref[0] = h_ref[...]
        cv_ref[0] = hist_ref[SUBLANES - (CONV_WIDTH - 1):, :]


def _mixer_ffn_prompt(z, x1, cos2, sin2, mk, mv, dmask, qdec, kdec, chunk_dec, small, wout, xn, wq, wo,
                      n2, wg, wu, wd, nf, tt):
    bsz, seq, _ = z.shape
    tps = seq // tt
    n_tiles = bsz * tps
    consts = [dmask, qdec, kdec, *small, wout, xn, wq, wo, n2, wg, wu, wd, nf]
    cur = lambda s: jnp.minimum(s, n_tiles - 1)
    prev = lambda s: jnp.maximum(s - 1, 0)
    seq_spec = lambda w: pl.BlockSpec((1, tt, w), lambda s: (cur(s) // tps, cur(s) % tps, 0))
    tab_spec = pl.BlockSpec((tt, RET_HEAD_DIM), lambda s: (cur(s) % tps, 0))
    mem_spec = pl.BlockSpec((1, N_MEM, D_MODEL), lambda s: (cur(s) // tps, 0, 0))
    state_spec = lambda *tail: pl.BlockSpec((1, *tail), lambda s: (cur(s) // tps,) + (0,) * len(tail))
    return pl.pallas_call(
        functools.partial(_mixer_ffn_prompt_kernel, tt=tt, tiles_per_seq=tps, n_tiles=n_tiles,
                          chunk_dec=chunk_dec),
        grid=(n_tiles + 1,),
        in_specs=[seq_spec(D_IN), seq_spec(D_MODEL), tab_spec, tab_spec, mem_spec, mem_spec]
        + [_const_spec(c.shape) for c in consts],
        out_specs=[pl.BlockSpec((1, tt, D_MODEL), lambda s: (prev(s) // tps, prev(s) % tps, 0)),
                   state_spec(RET_HEADS, RET_HEAD_DIM, RET_HEAD_DIM), state_spec(1, D_LRU),
                   state_spec(CONV_WIDTH - 1, D_LRU)],
        out_shape=[jax.ShapeDtypeStruct((bsz, seq, D_MODEL), F32),
                   jax.ShapeDtypeStruct((bsz, RET_HEADS, RET_HEAD_DIM, RET_HEAD_DIM), F32),
                   jax.ShapeDtypeStruct((bsz, 1, D_LRU), F32),
                   jax.ShapeDtypeStruct((bsz, CONV_WIDTH - 1, D_LRU), F32)],
        scratch_shapes=[pltpu.VMEM((SUBLANES, D_LRU), F32),
                        pltpu.VMEM((RET_HEADS, RET_HEAD_DIM, RET_HEAD_DIM), F32), pltpu.VMEM((1, D_LRU), F32),
                        pltpu.VMEM((tt, D_MODEL), F32)],
        compiler_params=_params(1),
        name="mixer_ffn_prompt",
    )(z, x1, cos2, sin2, mk, mv, *consts)


def _mixer_sample_kernel(z_ref, x1_ref, cs_ref, sn_ref, s0_ref, h0_ref, c0_ref, dmask_ref, qdec_ref, kdec_ref,
                         gn_ref, cw_ref, cb_ref, wgate_ref, ba_ref, bx_ref, lam_ref, ln_ref,
                         wout_ref, xn_ref, wq_ref,
                         x2_ref, q_ref, s_ref, h_ref, cv_ref,
                         mix_ref, qs_ref, ks_ref, kd_ref, vs_ref, cross_ref, hist_ref, *, g_seq, chunk_dec):
    rows_n = g_seq * SUBLANES
    cos2 = cs_ref[...]
    sin2 = sn_ref[...]
    for h in range(RET_HEADS):
        col = lambda j: slice(j * D_RET + h * RET_HEAD_DIM, j * D_RET + (h + 1) * RET_HEAD_DIM)
        q = _rotary(z_ref[:, col(0)], cos2, sin2)
        k = _rotary(z_ref[:, col(1)], cos2, sin2) * (RET_HEAD_DIM ** -0.5)
        qs_ref[h] = q
        ks_ref[h] = k
        kd_ref[h] = k * kdec_ref[h]
        vs_ref[h] = z_ref[:, col(2)]

    def seq_body(s, carry):
        r0 = pl.multiple_of(s * SUBLANES, SUBLANES)
        for h in range(RET_HEADS):
            s_old = s0_ref[s, h]
            q_bf = qs_ref[h, pl.ds(r0, SUBLANES), :].astype(BF16)
            cross_ref[h, pl.ds(r0, SUBLANES), :] = _dot(q_bf, s_old.astype(BF16))
            kd_bf = kd_ref[h, pl.ds(r0, SUBLANES), :].astype(BF16)
            v_bf = vs_ref[h, pl.ds(r0, SUBLANES), :].astype(BF16)
            s_ref[s, h] = chunk_dec[h] * s_old + lax.dot_general(
                kd_bf, v_bf, (((0,), (0,)), ((), ())), preferred_element_type=F32)
        return carry

    lax.fori_loop(0, g_seq, seq_body, 0)

    for h in range(RET_HEADS):
        col_g = slice(3 * D_RET + h * RET_HEAD_DIM, 3 * D_RET + (h + 1) * RET_HEAD_DIM)
        q_bf = qs_ref[h].astype(BF16)
        k_bf = ks_ref[h].astype(BF16)
        sc = lax.dot_general(q_bf, k_bf, (((1,), (1,)), ((), ())), preferred_element_type=F32) * dmask_ref[h]
        o = _dot(sc.astype(BF16), vs_ref[h].astype(BF16)) + cross_ref[h] * qdec_ref[h]
        gain = gn_ref[:, h * RET_HEAD_DIM:(h + 1) * RET_HEAD_DIM]
        mix_ref[:, h * RET_HEAD_DIM:(h + 1) * RET_HEAD_DIM] = _group_norm_gate(o, gain, z_ref[:, col_g]).astype(BF16)

    hist_ref[...] = jnp.zeros_like(hist_ref)
    hist_ref[:, SUBLANES - (CONV_WIDTH - 1):, :] = c0_ref[...]
    hs_groups = []
    for gi in range(D_LRU // LRU_GROUP):
        cols = slice(gi * LRU_GROUP, (gi + 1) * LRU_GROUP)
        u3 = z_ref[:, 4 * D_RET + gi * LRU_GROUP:4 * D_RET + (gi + 1) * LRU_GROUP].reshape(
            g_seq, SUBLANES, LRU_GROUP)
        cv_ref[:, :, cols] = u3[:, SUBLANES - (CONV_WIDTH - 1):, :]
        a3, gx3 = _lru_group(u3, hist_ref[:, :, cols], gi, cw_ref, cb_ref, wgate_ref, ba_ref, bx_ref, lam_ref)
        a_cum, h_loc = _slab_scan(a3, gx3)
        hs3 = h_loc + a_cum * h0_ref[:, :, cols]
        h_ref[:, :, cols] = hs3[:, SUBLANES - 1:, :]
        hs_groups.append(hs3.reshape(rows_n, LRU_GROUP))
    hs = jnp.concatenate(hs_groups, axis=-1)
    gate = z_ref[:, 4 * D_RET + D_LRU:]
    mix_ref[:, D_RET:] = (_rms(hs, ln_ref[...]) * _gelu_tanh(gate)).astype(BF16)

    x2 = x1_ref[...] + _dot(mix_ref[...], wout_ref[...])
    x2_ref[...] = x2
    hx = _rms(x2, xn_ref[...]).astype(BF16)
    q = (_dot(hx, wq_ref[...]) * (XATTN_HEAD_DIM ** -0.5)).reshape(g_seq, SUBLANES, D_MODEL)
    q_heads = [q[:, :, h * XATTN_HEAD_DIM:(h + 1) * XATTN_HEAD_DIM] for h in range(XATTN_HEADS)]
    q_ref[...] = jnp.concatenate(q_heads, axis=1).reshape(rows_n * XATTN_HEADS, XATTN_HEAD_DIM).astype(BF16)


def _mixer_sample(z, x1, cos2, sin2, s0, h0, c0, dmask, qdec, kdec, chunk_dec, small, wout, xn, wq, g_seq):
    rows = z.shape[0]
    n_seq = rows // SUBLANES
    rn = g_seq * SUBLANES
    consts = [dmask, qdec, kdec, *small, wout, xn, wq]
    row_spec = lambda w: pl.BlockSpec((rn, w), lambda i: (i, 0))
    st_spec = pl.BlockSpec((g_seq, RET_HEADS, RET_HEAD_DIM, RET_HEAD_DIM), lambda i: (i, 0, 0, 0))
    h_spec = pl.BlockSpec((g_seq, 1, D_LRU), lambda i: (i, 0, 0))
    c_spec = pl.BlockSpec((g_seq, CONV_WIDTH - 1, D_LRU), lambda i: (i, 0, 0))
    head_buf = pltpu.VMEM((RET_HEADS, rn, RET_HEAD_DIM), F32)
    return pl.pallas_call(
        functools.partial(_mixer_sample_kernel, g_seq=g_seq, chunk_dec=chunk_dec),
        grid=(n_seq // g_seq,),
        in_specs=[row_spec(D_IN), row_spec(D_MODEL), _const_spec(cos2.shape), _const_spec(sin2.shape),
                  st_spec, h_spec, c_spec] + [_const_spec(c.shape) for c in consts],
        out_specs=[row_spec(D_MODEL), pl.BlockSpec((rn * XATTN_HEADS, XATTN_HEAD_DIM), lambda i: (i, 0)),
                   st_spec, h_spec, c_spec],
        out_shape=[jax.ShapeDtypeStruct((rows, D_MODEL), F32),
                   jax.ShapeDtypeStruct((rows * XATTN_HEADS, XATTN_HEAD_DIM), BF16),
                   jax.ShapeDtypeStruct(s0.shape, F32), jax.ShapeDtypeStruct(h0.shape, F32),
                   jax.ShapeDtypeStruct(c0.shape, F32)],
        scratch_shapes=[pltpu.VMEM((rn, D_MODEL), BF16), head_buf, head_buf, head_buf, head_buf, head_buf,
                        pltpu.VMEM((g_seq, SUBLANES, D_LRU), F32)],
        compiler_params=_params(1),
        name="mixer_sample",
    )(z, x1, cos2, sin2, s0, h0, c0, *consts)


def _xattn_sample_kernel(q_ref, x2_ref, k_ref, v_ref, wo_ref, x3_ref, o_ref, *, g_seq):
    n_q = XATTN_HEADS * SUBLANES
    n_kv = N_MEM * XATTN_HEADS
    col_head = lax.broadcasted_iota(jnp.int32, (n_q, n_kv), 1) % XATTN_HEADS
    row_head = lax.broadcasted_iota(jnp.int32, (n_q, n_kv), 0) // SUBLANES
    own_head = col_head == row_head
    for s in range(g_seq):
        k_all = k_ref[s].reshape(n_kv, XATTN_HEAD_DIM).astype(BF16)
        v_all = v_ref[s].reshape(n_kv, XATTN_HEAD_DIM).astype(BF16)
        sc = lax.dot_general(q_ref[s * n_q:(s + 1) * n_q, :], k_all, (((1,), (1,)), ((), ())),
                             preferred_element_type=F32)
        sc = jnp.where(own_head, sc, -jnp.inf)
        p = jnp.exp(sc - jnp.max(sc, axis=-1, keepdims=True))
        o = _dot(p.astype(BF16), v_all) / jnp.sum(p, axis=-1, keepdims=True)
        for h in range(XATTN_HEADS):
            o_ref[s * SUBLANES:(s + 1) * SUBLANES, h * XATTN_HEAD_DIM:(h + 1) * XATTN_HEAD_DIM] = (
                o[h * SUBLANES:(h + 1) * SUBLANES, :])
    x3_ref[...] = x2_ref[...] + _dot(o_ref[...].astype(BF16), wo_ref[...])


def _xattn_sample(q, x2, ck, cv, wo, g_seq):
    rows = x2.shape[0]
    rn = g_seq * SUBLANES
    row_spec = pl.BlockSpec((rn, D_MODEL), lambda i: (i, 0))
    q_spec = pl.BlockSpec((rn * XATTN_HEADS, XATTN_HEAD_DIM), lambda i: (i, 0))
    mem_spec = pl.BlockSpec((None, g_seq, N_MEM, XATTN_HEADS, XATTN_HEAD_DIM), lambda i: (0, i, 0, 0, 0))
    return pl.pallas_call(
        functools.partial(_xattn_sample_kernel, g_seq=g_seq),
        grid=(rows // rn,),
        in_specs=[q_spec, row_spec, mem_spec, mem_spec, _const_spec(wo.shape)],
        out_specs=row_spec,
        out_shape=jax.ShapeDtypeStruct((rows, D_MODEL), F32),
        scratch_shapes=[pltpu.VMEM((rn, D_MODEL), F32)],
        compiler_params=_params(1),
        name="xattn_sample",
    )(q, x2, ck, cv, wo)


def _rope_tables(pos):
    half = RET_HEAD_DIM // 2
    inv = ROPE_BASE ** (-jnp.arange(half, dtype=F32) / half)
    ang = pos.astype(F32)[:, None] * inv[None, :]
    cos, sin = jnp.cos(ang), jnp.sin(ang)
    return jnp.concatenate([cos, cos], axis=-1), jnp.concatenate([-sin, sin], axis=-1)


def _decay_tables(c, n_rep):
    lg = jnp.log(1.0 - 2.0 ** (-5.0 - jnp.arange(RET_HEADS, dtype=F32)))
    idx = jnp.arange(c, dtype=F32)
    diff = idx[:, None] - idx[None, :]
    dmask = jnp.where(diff[None] >= 0, jnp.exp(lg[:, None, None] * jnp.maximum(diff, 0.0)[None]), 0.0)
    q_dec = jnp.exp(lg[:, None] * (idx[None, :] + 1.0))
    k_dec = jnp.exp(lg[:, None] * (c - 1.0 - idx[None, :]))
    n = c * n_rep
    if n_rep > 1:
        eye = jnp.eye(n_rep, dtype=F32)
        dmask = jnp.einsum('ab,hij->haibj', eye, dmask).reshape(RET_HEADS, n, n)
        q_dec = jnp.tile(q_dec, (1, n_rep))
        k_dec = jnp.tile(k_dec, (1, n_rep))
    bcast = lambda d: jnp.broadcast_to(d[:, :, None], (RET_HEADS, n, RET_HEAD_DIM))
    gammas = [1.0 - 2.0 ** (-5.0 - h) for h in range(RET_HEADS)]
    chunk_dec = tuple(g ** c for g in gammas)
    return dmask, bcast(q_dec), bcast(k_dec), chunk_dec


def _gate_weights(wa, wx):
    per = LRU_GROUP // LRU_BLOCK_DIM
    eye = jnp.eye(per, dtype=wa.dtype)

    def grouped(w):
        w = w.reshape(LRU_BLOCKS // per, per, LRU_BLOCK_DIM, LRU_BLOCK_DIM)
        return jnp.einsum('ab,gakj->gakbj', eye, w).reshape(LRU_BLOCKS // per, LRU_GROUP, LRU_GROUP)

    return jnp.concatenate([grouped(wa), grouped(wx)], axis=-1).astype(BF16)


def kernel(x_prompt, x_sample, state_ret, state_lru_h, state_lru_conv, cache_mem_k, cache_mem_v, mem_prompt,
           ffn1_norm, ffn1_wg, ffn1_wu, ffn1_wd, mix_norm, w_in, ret_gn_gain, conv_w, conv_b,
           lru_wa, lru_ba, lru_wx, lru_bx, lru_lambda, lru_norm, w_out,
           xattn_norm, xattn_wq, xattn_wk, xattn_wv, xattn_wo,
           ffn2_norm, ffn2_wg, ffn2_wu, ffn2_wd, final_norm):
    depth = ffn1_wg.shape[0]
    assert depth == 1, "single-layer trunk"
    bp, tp, _ = x_prompt.shape
    bs, ts, _ = x_sample.shape
    assert ts == SUBLANES and tp % RET_CHUNK == 0
    bf = lambda w: w[0].astype(BF16)
    row = lambda v: v.reshape(1, -1)

    f1 = (row(ffn1_norm[0]), bf(ffn1_wg), bf(ffn1_wu), bf(ffn1_wd), row(mix_norm[0]), bf(w_in))
    f2 = (row(ffn2_norm[0]), bf(ffn2_wg), bf(ffn2_wu), bf(ffn2_wd), row(final_norm))
    small = [row(ret_gn_gain[0]), conv_w[0], row(conv_b[0]), _gate_weights(lru_wa[0], lru_wx[0]),
             row(lru_ba[0]), row(lru_bx[0]), row(lru_lambda[0]), row(lru_norm[0])]
    wout, xn, wq, wo = bf(w_out), row(xattn_norm[0]), bf(xattn_wq), bf(xattn_wo)

    tm = 256
    mk, mv, mk_bf, mv_bf = _mem_kv(mem_prompt, bf(xattn_wk), bf(xattn_wv))
    x1p, zp = _ffn_in(x_prompt.reshape(bp * tp, D_MODEL), *f1, 2 * tm)
    cos_p, sin_p = _rope_tables(jnp.arange(tp, dtype=jnp.int32))
    tt = 256
    dmask, qdec, kdec, cdec = _decay_tables(tt, 1)
    y_p, s_p, h_p, c_p = _mixer_ffn_prompt(
        zp.reshape(bp, tp, D_IN), x1p.reshape(bp, tp, D_MODEL), cos_p, sin_p,
        mk_bf, mv_bf, dmask, qdec, kdec, cdec, small, wout, xn, wq, wo, *f2, tt)

    g_seq = 16
    x1s, zs = _ffn_in(x_sample.reshape(bs * ts, D_MODEL), *f1, tm)
    cos_s, sin_s = _rope_tables(PAST_LEN + jnp.arange(ts, dtype=jnp.int32))
    cos_s, sin_s = jnp.tile(cos_s, (g_seq, 1)), jnp.tile(sin_s, (g_seq, 1))
    dmask8, qdec8, kdec8, cdec8 = _decay_tables(ts, g_seq)
    x2s, qs, s_s, h_s, c_s = _mixer_sample(
        zs, x1s, cos_s, sin_s, state_ret[0], state_lru_h[0].reshape(bs, 1, D_LRU), state_lru_conv[0],
        dmask8, qdec8, kdec8, cdec8, small, wout, xn, wq, g_seq)
    g_att = 8
    x3s = _xattn_sample(qs, x2s, cache_mem_k, cache_mem_v, wo, g_att)
    y_s = _ffn_out(x3s, *f2, tm).reshape(bs, ts, D_MODEL)

    return (y_p, y_s, s_p[None], h_p.reshape(1, bp, D_LRU), c_p[None],
            mk, mv, s_s[None], h_s.reshape(1, bs, D_LRU), c_s[None])
```

```python
import functools
import math

import jax
import jax.numpy as jnp
from jax import lax
from jax.experimental import pallas as pl
from jax.experimental.pallas import tpu as pltpu

D_MODEL = 1024
PAST_LEN = 16384
N_MEM = 256
D_RET = 512
D_LRU = 512
RET_HEADS = 4
RET_HEAD_DIM = 128
RET_CHUNK = 128
ROPE_BASE = 10000.0
LRU_BLOCKS = 8
LRU_BLOCK_DIM = 64
CONV_WIDTH = 4
LRU_C = 8.0
XATTN_HEADS = 4
XATTN_HEAD_DIM = 256
D_FF = 2816
D_IN = 3072
EPS = 1e-6

SUBLANES = 8
LANES = 128
FFN_CHUNK = 256
LRU_GROUP = 128
VMEM_LIMIT_BYTES = 56 * 1024 * 1024

F32 = jnp.float32
BF16 = jnp.bfloat16


def _rms(x, g):
    ms = jnp.mean(x * x, axis=-1, keepdims=True)
    return x * lax.rsqrt(ms + EPS) * g


def _dot(a, b):
    return jnp.dot(a, b, preferred_element_type=F32)


def _swiglu(h_bf, wg_ref, wu_ref, wd_ref):
    g = _dot(h_bf, wg_ref[...])
    u = _dot(h_bf, wu_ref[...])
    a = (g * jax.nn.sigmoid(g)) * u
    return _dot(a.astype(BF16), wd_ref[...])


def _const_spec(shape):
    nd = len(shape)
    return pl.BlockSpec(shape, lambda *_: (0,) * nd)


def _params(n_axes):
    return pltpu.CompilerParams(dimension_semantics=("arbitrary",) * n_axes,
                                vmem_limit_bytes=VMEM_LIMIT_BYTES)


def _ffn_in_kernel(x_ref, n1_ref, wg_ref, wu_ref, wd_ref, n2_ref, win_ref, x1_ref, z_ref):
    x = x_ref[...]
    h = _rms(x, n1_ref[...]).astype(BF16)
    x1 = x + 0.5 * _swiglu(h, wg_ref, wu_ref, wd_ref)
    x1_ref[...] = x1
    hn = _rms(x1, n2_ref[...]).astype(BF16)
    z_ref[...] = _dot(hn, win_ref[...])


def _ffn_in(x, n1, wg, wu, wd, n2, win, tm):
    rows = x.shape[0]
    row_spec = lambda w: pl.BlockSpec((tm, w), lambda i: (i, 0))
    return pl.pallas_call(
        _ffn_in_kernel,
        grid=(rows // tm,),
        in_specs=[row_spec(D_MODEL), _const_spec(n1.shape), _const_spec(wg.shape), _const_spec(wu.shape),
                  _const_spec(wd.shape), _const_spec(n2.shape), _const_spec(win.shape)],
        out_specs=[row_spec(D_MODEL), row_spec(D_IN)],
        out_shape=[jax.ShapeDtypeStruct((rows, D_MODEL), F32), jax.ShapeDtypeStruct((rows, D_IN), F32)],
        compiler_params=_params(1),
        name="ffn_in",
    )(x, n1, wg, wu, wd, n2, win)


def _ffn_out_kernel(x_ref, n1_ref, wg_ref, wu_ref, wd_ref, nf_ref, y_ref):
    x = x_ref[...]
    h = _rms(x, n1_ref[...]).astype(BF16)
    x4 = x + 0.5 * _swiglu(h, wg_ref, wu_ref, wd_ref)
    y_ref[...] = _rms(x4, nf_ref[...])


def _ffn_out(x, n1, wg, wu, wd, nf, tm):
    rows = x.shape[0]
    row_spec = pl.BlockSpec((tm, D_MODEL), lambda i: (i, 0))
    return pl.pallas_call(
        _ffn_out_kernel,
        grid=(rows // tm,),
        in_specs=[row_spec, _const_spec(n1.shape), _const_spec(wg.shape), _const_spec(wu.shape),
                  _const_spec(wd.shape), _const_spec(nf.shape)],
        out_specs=row_spec,
        out_shape=jax.ShapeDtypeStruct((rows, D_MODEL), F32),
        compiler_params=_params(1),
        name="ffn_out",
    )(x, n1, wg, wu, wd, nf)


def _mem_kv_kernel(m_ref, wk_ref, wv_ref, k_ref, v_ref, kb_ref, vb_ref):
    m = m_ref[0].astype(BF16)
    for w_ref, o_ref, ob_ref in ((wk_ref, k_ref, kb_ref), (wv_ref, v_ref, vb_ref)):
        kv = _dot(m, w_ref[...])
        ob_ref[0] = kv.astype(BF16)
        for h in range(XATTN_HEADS):
            o_ref[:, h, :] = kv[:, h * XATTN_HEAD_DIM:(h + 1) * XATTN_HEAD_DIM]


def _mem_kv(mem, wk, wv):
    bsz = mem.shape[0]
    row_spec = pl.BlockSpec((1, N_MEM, D_MODEL), lambda b: (b, 0, 0))
    cache_spec = pl.BlockSpec((None, None, N_MEM, XATTN_HEADS, XATTN_HEAD_DIM), lambda b: (0, b, 0, 0, 0))
    cache_shape = jax.ShapeDtypeStruct((1, bsz, N_MEM, XATTN_HEADS, XATTN_HEAD_DIM), F32)
    return pl.pallas_call(
        _mem_kv_kernel,
        grid=(bsz,),
        in_specs=[row_spec, _const_spec(wk.shape), _const_spec(wv.shape)],
        out_specs=[cache_spec, cache_spec, row_spec, row_spec],
        out_shape=[cache_shape, cache_shape] + [jax.ShapeDtypeStruct((bsz, N_MEM, D_MODEL), BF16)] * 2,
        compiler_params=_params(1),
        name="mem_kv",
    )(mem, wk, wv)


def _rotary(x, cos2, sin2):
    return x * cos2 + pltpu.roll(x, RET_HEAD_DIM // 2, axis=1) * sin2


def _group_norm_gate(o, gain, g):
    mu = jnp.mean(o, axis=-1, keepdims=True)
    oc = o - mu
    var = jnp.mean(oc * oc, axis=-1, keepdims=True)
    on = oc * lax.rsqrt(var + EPS) * gain
    return (g * jax.nn.sigmoid(g)) * on


def _softplus(x):
    return jnp.maximum(x, 0.0) + jnp.log1p(jnp.exp(-jnp.abs(x)))


def _lru_group(u3, hist3, gi, cw_ref, cb_ref, wgate_ref, ba_ref, bx_ref, lam_ref):
    cols = slice(gi * LRU_GROUP, (gi + 1) * LRU_GROUP)
    uc3 = _conv_taps(u3, hist3, cw_ref[:, cols], cb_ref[:, cols])
    uc = uc3.reshape(u3.shape[0] * SUBLANES, LRU_GROUP)
    pre = _dot(uc.astype(BF16), wgate_ref[gi])
    r = jax.nn.sigmoid(pre[:, :LRU_GROUP] + ba_ref[:, cols])
    i = jax.nn.sigmoid(pre[:, LRU_GROUP:] + bx_ref[:, cols])
    log_a = (-LRU_C) * r * _softplus(-lam_ref[:, cols])
    a = jnp.exp(log_a)
    gx = jnp.sqrt(1.0 - a * a) * (i * uc)
    return a.reshape(u3.shape), gx.reshape(u3.shape)


def _slab_scan(a, g):
    t = lax.broadcasted_iota(jnp.int32, a.shape, 1)
    d = 1
    while d < SUBLANES:
        keep = t >= d
        g_prev = jnp.where(keep, pltpu.roll(g, d, axis=1), 0.0)
        a_prev = jnp.where(keep, pltpu.roll(a, d, axis=1), 1.0)
        g = g + a * g_prev
        a = a * a_prev
        d *= 2
    return a, g


def _gelu_tanh(x):
    return 0.5 * x * (1.0 + jnp.tanh(math.sqrt(2.0 / math.pi) * (x + 0.044715 * (x * x * x))))


def _conv_taps(u3, hist3, cw, cb):
    t = lax.broadcasted_iota(jnp.int32, u3.shape, 1)
    acc = u3 * cw[CONV_WIDTH - 1:CONV_WIDTH, :] + cb
    for k in range(1, CONV_WIDTH):
        prev = jnp.where(t >= k, pltpu.roll(u3, k, axis=1), pltpu.roll(hist3, k, axis=1))
        acc = acc + prev * cw[CONV_WIDTH - 1 - k:CONV_WIDTH - k, :]
    return acc


def _attention_head(q_bf, k_bf, v_bf):
    sc = lax.dot_general(q_bf, k_bf, (((1,), (1,)), ((), ())), preferred_element_type=F32)
    p = jnp.exp(sc - jnp.max(sc, axis=-1, keepdims=True))
    l = jnp.sum(p, axis=-1, keepdims=True)
    return (_dot(p.astype(BF16), v_bf) / l).astype(BF16)


def _interleave(a, b):
    out, taken = [], 0
    for i, unit in enumerate(a):
        out.append(unit)
        want = (i + 1) * len(b) // len(a)
        out.extend(b[taken:want])
        taken = want
    return out


def _mixer_ffn_prompt_kernel(z_ref, x1_ref, cs_ref, sn_ref, mk_ref, mv_ref, dmask_ref, qdec_ref, kdec_ref,
                             gn_ref, cw_ref, cb_ref, wgate_ref, ba_ref, bx_ref, lam_ref, ln_ref,
                             wout_ref, xn_ref, wq_ref, wo_ref, n2_ref, wg_ref, wu_ref, wd_ref, nf_ref,
                             y_ref, so_ref, ho_ref, cv_ref,
                             hist_ref, s_ref, h_ref, x3_ref,
                             *, tt, tiles_per_seq, n_tiles, chunk_dec):
    step = pl.program_id(0)
    t_idx = jnp.minimum(step, n_tiles - 1) % tiles_per_seq

    @pl.when(step == 0)
    def _():
        x3_ref[...] = jnp.zeros_like(x3_ref)

    @pl.when(t_idx == 0)
    def _():
        s_ref[...] = jnp.zeros_like(s_ref)
        h_ref[...] = jnp.zeros_like(h_ref)
        hist_ref[...] = jnp.zeros_like(hist_ref)

    x3_prev = x3_ref[...]
    hf = _rms(x3_prev, n2_ref[...]).astype(BF16)
    ffn = {"act": [], "x4": []}

    def ffn_up_chunk(j):
        cols = slice(j * FFN_CHUNK, (j + 1) * FFN_CHUNK)
        g = _dot(hf, wg_ref[:, cols])
        u = _dot(hf, wu_ref[:, cols])
        ffn["act"].append(((g * jax.nn.sigmoid(g)) * u).astype(BF16))

    def ffn_down_chunk(j):
        if j == 0:
            ffn["act"] = jnp.concatenate(ffn["act"], axis=-1)
        cols = slice(j * 2 * FFN_CHUNK, (j + 1) * 2 * FFN_CHUNK)
        ffn["x4"].append(x3_prev[:, cols] + 0.5 * _dot(ffn["act"], wd_ref[:, cols]))

    def ffn_finish():
        y_ref[0] = _rms(jnp.concatenate(ffn["x4"], axis=-1), nf_ref[...])

    ffn_units = [functools.partial(ffn_up_chunk, j) for j in range(D_FF // FFN_CHUNK)]
    ffn_units += [functools.partial(ffn_down_chunk, j) for j in range(D_MODEL // (2 * FFN_CHUNK))] + [ffn_finish]

    mx = {"ret": [], "hs": [], "hs_sq": [], "lru": []}

    def retention_unit(pair):
        heads = (2 * pair, 2 * pair + 1)
        cos2 = cs_ref[...]
        sin2 = sn_ref[...]
        col = lambda j, h: slice(j * D_RET + h * RET_HEAD_DIM, j * D_RET + (h + 1) * RET_HEAD_DIM)
        q_bf = [_rotary(z_ref[0, :, col(0, h)], cos2, sin2).astype(BF16) for h in heads]
        k = [_rotary(z_ref[0, :, col(1, h)], cos2, sin2) * (RET_HEAD_DIM ** -0.5) for h in heads]
        v_bf = [z_ref[0, :, col(2, h)].astype(BF16) for h in heads]
        sc = [lax.dot_general(q_bf[i], k[i].astype(BF16), (((1,), (1,)), ((), ())),
                              preferred_element_type=F32) * dmask_ref[h] for i, h in enumerate(heads)]
        s_old = [s_ref[h] for h in heads]
        cross = [_dot(q_bf[i], s_old[i].astype(BF16)) * qdec_ref[h] for i, h in enumerate(heads)]
        o = [_dot(sc[i].astype(BF16), v_bf[i]) + cross[i] for i in range(2)]
        kd_bf = [(k[i] * kdec_ref[h]).astype(BF16) for i, h in enumerate(heads)]
        for i, h in enumerate(heads):
            s_ref[h] = chunk_dec[h] * s_old[i] + lax.dot_general(
                kd_bf[i], v_bf[i], (((0,), (0,)), ((), ())), preferred_element_type=F32)
        for i, h in enumerate(heads):
            gain = gn_ref[:, h * RET_HEAD_DIM:(h + 1) * RET_HEAD_DIM]
            mx["ret"].append(_group_norm_gate(o[i], gain, z_ref[0, :, col(3, h)]).astype(BF16))

    n_slab = tt // SUBLANES

    def lru_group_unit(gi):
        cols = slice(gi * LRU_GROUP, (gi + 1) * LRU_GROUP)
        u3 = z_ref[0, :, 4 * D_RET + gi * LRU_GROUP:4 * D_RET + (gi + 1) * LRU_GROUP].reshape(
            n_slab, SUBLANES, LRU_GROUP)
        hist3 = jnp.concatenate([hist_ref[:, cols].reshape(1, SUBLANES, LRU_GROUP), u3[:-1]], axis=0)
        hist_ref[:, cols] = u3[n_slab - 1]
        a3, gx3 = _lru_group(u3, hist3, gi, cw_ref, cb_ref, wgate_ref, ba_ref, bx_ref, lam_ref)
        a_cum, h_loc = _slab_scan(a3, gx3)
        carry = h_ref[:, cols]
        slabs = []
        for s in range(n_slab):
            hs_s = h_loc[s] + a_cum[s] * carry
            slabs.append(hs_s)
            carry = hs_s[SUBLANES - 1:, :]
        h_ref[:, cols] = carry
        hs = jnp.concatenate(slabs, axis=0)
        mx["hs"].append(hs)
        mx["hs_sq"].append(jnp.sum(hs * hs, axis=-1, keepdims=True))

    def lru_norm_unit():
        inv = lax.rsqrt(sum(mx["hs_sq"]) * (1.0 / D_LRU) + EPS)
        for gi, hs in enumerate(mx["hs"]):
            cols = slice(gi * LRU_GROUP, (gi + 1) * LRU_GROUP)
            gate = z_ref[0, :, 4 * D_RET + D_LRU + gi * LRU_GROUP:4 * D_RET + D_LRU + (gi + 1) * LRU_GROUP]
            mx["lru"].append((hs * inv * ln_ref[:, cols] * _gelu_tanh(gate)).astype(BF16))

    def out_proj_unit():
        mix = jnp.concatenate(mx["ret"] + mx["lru"], axis=-1)
        x2 = x1_ref[0] + _dot(mix, wout_ref[...])
        hx = _rms(x2, xn_ref[...]).astype(BF16)
        mx["x2"] = x2
        mx["q"] = (_dot(hx, wq_ref[...]) * (XATTN_HEAD_DIM ** -0.5)).astype(BF16)
        mx["o"] = []

    def attention_unit(pair):
        sls = [slice(h * XATTN_HEAD_DIM, (h + 1) * XATTN_HEAD_DIM) for h in (2 * pair, 2 * pair + 1)]
        sc = [lax.dot_general(mx["q"][:, sl], mk_ref[0, :, sl], (((1,), (1,)), ((), ())),
                              preferred_element_type=F32) for sl in sls]
        p = [jnp.exp(s - jnp.max(s, axis=-1, keepdims=True)) for s in sc]
        l = [jnp.sum(pi, axis=-1, keepdims=True) for pi in p]
        o = [_dot(pi.astype(BF16), mv_ref[0, :, sl]) for pi, sl in zip(p, sls)]
        mx["o"] += [(oi / li).astype(BF16) for oi, li in zip(o, l)]

    def attn_out_unit():
        x3_ref[...] = mx["x2"] + _dot(jnp.concatenate(mx["o"], axis=-1), wo_ref[...])

    mixer_units = [functools.partial(retention_unit, p) for p in range(RET_HEADS // 2)]
    mixer_units += [functools.partial(lru_group_unit, gi) for gi in range(D_LRU // LRU_GROUP)]
    mixer_units += [lru_norm_unit, out_proj_unit]
    mixer_units += [functools.partial(attention_unit, p) for p in range(XATTN_HEADS // 2)]

    for unit in _interleave(ffn_units, mixer_units):
        unit()
    attn_out_unit()

    @pl.when(jnp.logical_and(t_idx == tiles_per_seq - 1, step < n_tiles))
    def _():
        so_ref[0] = s_ref[...]
        ho_ref[0] = h_ref[...]
        cv_ref[0] = hist_ref[SUBLANES - (CONV_WIDTH - 1):, :]


def _mixer_ffn_prompt(z, x1, cos2, sin2, mk, mv, dmask, qdec, kdec, chunk_dec, small, wout, xn, wq, wo,
                      n2, wg, wu, wd, nf, tt):
    bsz, seq, _ = z.shape
    tps = seq // tt
    n_tiles = bsz * tps
    consts = [dmask, qdec, kdec, *small, wout, xn, wq, wo, n2, wg, wu, wd, nf]
    cur = lambda s: jnp.minimum(s, n_tiles - 1)
    prev = lambda s: jnp.maximum(s - 1, 0)
    seq_spec = lambda w: pl.BlockSpec((1, tt, w), lambda s: (cur(s) // tps, cur(s) % tps, 0))
    tab_spec = pl.BlockSpec((tt, RET_HEAD_DIM), lambda s: (cur(s) % tps, 0))
    mem_spec = pl.BlockSpec((1, N_MEM, D_MODEL), lambda s: (cur(s) // tps, 0, 0))
    state_spec = lambda *tail: pl.BlockSpec((1, *tail), lambda s: (cur(s) // tps,) + (0,) * len(tail))
    return pl.pallas_call(
        functools.partial(_mixer_ffn_prompt_kernel, tt=tt, tiles_per_seq=tps, n_tiles=n_tiles,
                          chunk_dec=chunk_dec),
        grid=(n_tiles + 1,),
        in_specs=[seq_spec(D_IN), seq_spec(D_MODEL), tab_spec, tab_spec, mem_spec, mem_spec]
        + [_const_spec(c.shape) for c in consts],
        out_specs=[pl.BlockSpec((1, tt, D_MODEL), lambda s: (prev(s) // tps, prev(s) % tps, 0)),
                   state_spec(RET_HEADS, RET_HEAD_DIM, RET_HEAD_DIM), state_spec(1, D_LRU),
                   state_spec(CONV_WIDTH - 1, D_LRU)],
        out_shape=[jax.ShapeDtypeStruct((bsz, seq, D_MODEL), F32),
                   jax.ShapeDtypeStruct((bsz, RET_HEADS, RET_HEAD_DIM, RET_HEAD_DIM), F32),
                   jax.ShapeDtypeStruct((bsz, 1, D_LRU), F32),
                   jax.ShapeDtypeStruct((bsz, CONV_WIDTH - 1, D_LRU), F32)],
        scratch_shapes=[pltpu.VMEM((SUBLANES, D_LRU), F32),
                        pltpu.VMEM((RET_HEADS, RET_HEAD_DIM, RET_HEAD_DIM), F32), pltpu.VMEM((1, D_LRU), F32),
                        pltpu.VMEM((tt, D_MODEL), F32)],
        compiler_params=_params(1),
        name="mixer_ffn_prompt",
    )(z, x1, cos2, sin2, mk, mv, *consts)


def _mixer_sample_kernel(z_ref, x1_ref, cs_ref, sn_ref, s0_ref, h0_ref, c0_ref, dmask_ref, qdec_ref, kdec_ref,
                         gn_ref, cw_ref, cb_ref, wgate_ref, ba_ref, bx_ref, lam_ref, ln_ref,
                         wout_ref, xn_ref, wq_ref,
                         x2_ref, q_ref, s_ref, h_ref, cv_ref,
                         mix_ref, qs_ref, ks_ref, kd_ref, vs_ref, cross_ref, hist_ref, *, g_seq, chunk_dec):
    rows_n = g_seq * SUBLANES
    cos2 = cs_ref[...]
    sin2 = sn_ref[...]
    for h in range(RET_HEADS):
        col = lambda j: slice(j * D_RET + h * RET_HEAD_DIM, j * D_RET + (h + 1) * RET_HEAD_DIM)
        q = _rotary(z_ref[:, col(0)], cos2, sin2)
        k = _rotary(z_ref[:, col(1)], cos2, sin2) * (RET_HEAD_DIM ** -0.5)
        qs_ref[h] = q
        ks_ref[h] = k
        kd_ref[h] = k * kdec_ref[h]
        vs_ref[h] = z_ref[:, col(2)]

    def seq_body(s, carry):
        r0 = pl.multiple_of(s * SUBLANES, SUBLANES)
        for h in range(RET_HEADS):
            s_old = s0_ref[s, h]
            q_bf = qs_ref[h, pl.ds(r0, SUBLANES), :].astype(BF16)
            cross_ref[h, pl.ds(r0, SUBLANES), :] = _dot(q_bf, s_old.astype(BF16))
            kd_bf = kd_ref[h, pl.ds(r0, SUBLANES), :].astype(BF16)
            v_bf = vs_ref[h, pl.ds(r0, SUBLANES), :].astype(BF16)
            s_ref[s, h] = chunk_dec[h] * s_old + lax.dot_general(
                kd_bf, v_bf, (((0,), (0,)), ((), ())), preferred_element_type=F32)
        return carry

    lax.fori_loop(0, g_seq, seq_body, 0, unroll=8)

    for h in range(RET_HEADS):
        col_g = slice(3 * D_RET + h * RET_HEAD_DIM, 3 * D_RET + (h + 1) * RET_HEAD_DIM)
        q_bf = qs_ref[h].astype(BF16)
        k_bf = ks_ref[h].astype(BF16)
        sc = lax.dot_general(q_bf, k_bf, (((1,), (1,)), ((), ())), preferred_element_type=F32) * dmask_ref[h]
        o = _dot(sc.astype(BF16), vs_ref[h].astype(BF16)) + cross_ref[h] * qdec_ref[h]
        gain = gn_ref[:, h * RET_HEAD_DIM:(h + 1) * RET_HEAD_DIM]
        mix_ref[:, h * RET_HEAD_DIM:(h + 1) * RET_HEAD_DIM] = _group_norm_gate(o, gain, z_ref[:, col_g]).astype(BF16)

    hist_ref[...] = jnp.zeros_like(hist_ref)
    hist_ref[:, SUBLANES - (CONV_WIDTH - 1):, :] = c0_ref[...]
    hs_groups = []
    for gi in range(D_LRU // LRU_GROUP):
        cols = slice(gi * LRU_GROUP, (gi + 1) * LRU_GROUP)
        u3 = z_ref[:, 4 * D_RET + gi * LRU_GROUP:4 * D_RET + (gi + 1) * LRU_GROUP].reshape(
            g_seq, SUBLANES, LRU_GROUP)
        cv_ref[:, :, cols] = u3[:, SUBLANES - (CONV_WIDTH - 1):, :]
        a3, gx3 = _lru_group(u3, hist_ref[:, :, cols], gi, cw_ref, cb_ref, wgate_ref, ba_ref, bx_ref, lam_ref)
        a_cum, h_loc = _slab_scan(a3, gx3)
        hs3 = h_loc + a_cum * h0_ref[:, :, cols]
        h_ref[:, :, cols] = hs3[:, SUBLANES - 1:, :]
        hs_groups.append(hs3.reshape(rows_n, LRU_GROUP))
    hs = jnp.concatenate(hs_groups, axis=-1)
    gate = z_ref[:, 4 * D_RET + D_LRU:]
    mix_ref[:, D_RET:] = (_rms(hs, ln_ref[...]) * _gelu_tanh(gate)).astype(BF16)

    x2 = x1_ref[...] + _dot(mix_ref[...], wout_ref[...])
    x2_ref[...] = x2
    hx = _rms(x2, xn_ref[...]).astype(BF16)
    q = (_dot(hx, wq_ref[...]) * (XATTN_HEAD_DIM ** -0.5)).reshape(g_seq, SUBLANES, D_MODEL)
    q_heads = [q[:, :, h * XATTN_HEAD_DIM:(h + 1) * XATTN_HEAD_DIM] for h in range(XATTN_HEADS)]
    q_ref[...] = jnp.concatenate(q_heads, axis=1).reshape(rows_n * XATTN_HEADS, XATTN_HEAD_DIM).astype(BF16)


def _mixer_sample(z, x1, cos2, sin2, s0, h0, c0, dmask, qdec, kdec, chunk_dec, small, wout, xn, wq, g_seq):
    rows = z.shape[0]
    n_seq = rows // SUBLANES
    rn = g_seq * SUBLANES
    consts = [dmask, qdec, kdec, *small, wout, xn, wq]
    row_spec = lambda w: pl.BlockSpec((rn, w), lambda i: (i, 0))
    st_spec = pl.BlockSpec((g_seq, RET_HEADS, RET_HEAD_DIM, RET_HEAD_DIM), lambda i: (i, 0, 0, 0))
    h_spec = pl.BlockSpec((g_seq, 1, D_LRU), lambda i: (i, 0, 0))
    c_spec = pl.BlockSpec((g_seq, CONV_WIDTH - 1, D_LRU), lambda i: (i, 0, 0))
    head_buf = pltpu.VMEM((RET_HEADS, rn, RET_HEAD_DIM), F32)
    return pl.pallas_call(
        functools.partial(_mixer_sample_kernel, g_seq=g_seq, chunk_dec=chunk_dec),
        grid=(n_seq // g_seq,),
        in_specs=[row_spec(D_IN), row_spec(D_MODEL), _const_spec(cos2.shape), _const_spec(sin2.shape),
                  st_spec, h_spec, c_spec] + [_const_spec(c.shape) for c in consts],
        out_specs=[row_spec(D_MODEL), pl.BlockSpec((rn * XATTN_HEADS, XATTN_HEAD_DIM), lambda i: (i, 0)),
                   st_spec, h_spec, c_spec],
        out_shape=[jax.ShapeDtypeStruct((rows, D_MODEL), F32),
                   jax.ShapeDtypeStruct((rows * XATTN_HEADS, XATTN_HEAD_DIM), BF16),
                   jax.ShapeDtypeStruct(s0.shape, F32), jax.ShapeDtypeStruct(h0.shape, F32),
                   jax.ShapeDtypeStruct(c0.shape, F32)],
        scratch_shapes=[pltpu.VMEM((rn, D_MODEL), BF16), head_buf, head_buf, head_buf, head_buf, head_buf,
                        pltpu.VMEM((g_seq, SUBLANES, D_LRU), F32)],
        compiler_params=_params(1),
        name="mixer_sample",
    )(z, x1, cos2, sin2, s0, h0, c0, *consts)


def _xattn_sample_kernel(q_ref, x2_ref, k_ref, v_ref, wo_ref, x3_ref, o_ref, *, g_seq):
    n_q = XATTN_HEADS * SUBLANES
    n_kv = N_MEM * XATTN_HEADS
    col_head = lax.broadcasted_iota(jnp.int32, (n_q, n_kv), 1) % XATTN_HEADS
    row_head = lax.broadcasted_iota(jnp.int32, (n_q, n_kv), 0) // SUBLANES
    own_head = col_head == row_head
    for s in range(g_seq):
        k_all = k_ref[s].reshape(n_kv, XATTN_HEAD_DIM).astype(BF16)
        v_all = v_ref[s].reshape(n_kv, XATTN_HEAD_DIM).astype(BF16)
        sc = lax.dot_general(q_ref[s * n_q:(s + 1) * n_q, :], k_all, (((1,), (1,)), ((), ())),
                             preferred_element_type=F32)
        sc = jnp.where(own_head, sc, -jnp.inf)
        p = jnp.exp(sc - jnp.max(sc, axis=-1, keepdims=True))
        o = _dot(p.astype(BF16), v_all) / jnp.sum(p, axis=-1, keepdims=True)
        for h in range(XATTN_HEADS):
            o_ref[s * SUBLANES:(s + 1) * SUBLANES, h * XATTN_HEAD_DIM:(h + 1) * XATTN_HEAD_DIM] = (
                o[h * SUBLANES:(h + 1) * SUBLANES, :])
    x3_ref[...] = x2_ref[...] + _dot(o_ref[...].astype(BF16), wo_ref[...])


def _xattn_sample(q, x2, ck, cv, wo, g_seq):
    rows = x2.shape[0]
    rn = g_seq * SUBLANES
    row_spec = pl.BlockSpec((rn, D_MODEL), lambda i: (i, 0))
    q_spec = pl.BlockSpec((rn * XATTN_HEADS, XATTN_HEAD_DIM), lambda i: (i, 0))
    mem_spec = pl.BlockSpec((None, g_seq, N_MEM, XATTN_HEADS, XATTN_HEAD_DIM), lambda i: (0, i, 0, 0, 0))
    return pl.pallas_call(
        functools.partial(_xattn_sample_kernel, g_seq=g_seq),
        grid=(rows // rn,),
        in_specs=[q_spec, row_spec, mem_spec, mem_spec, _const_spec(wo.shape)],
        out_specs=row_spec,
        out_shape=jax.ShapeDtypeStruct((rows, D_MODEL), F32),
        scratch_shapes=[pltpu.VMEM((rn, D_MODEL), F32)],
        compiler_params=_params(1),
        name="xattn_sample",
    )(q, x2, ck, cv, wo)


def _rope_tables(pos):
    half = RET_HEAD_DIM // 2
    inv = ROPE_BASE ** (-jnp.arange(half, dtype=F32) / half)
    ang = pos.astype(F32)[:, None] * inv[None, :]
    cos, sin = jnp.cos(ang), jnp.sin(ang)
    return jnp.concatenate([cos, cos], axis=-1), jnp.concatenate([-sin, sin], axis=-1)


def _decay_tables(c, n_rep):
    lg = jnp.log(1.0 - 2.0 ** (-5.0 - jnp.arange(RET_HEADS, dtype=F32)))
    idx = jnp.arange(c, dtype=F32)
    diff = idx[:, None] - idx[None, :]
    dmask = jnp.where(diff[None] >= 0, jnp.exp(lg[:, None, None] * jnp.maximum(diff, 0.0)[None]), 0.0)
    q_dec = jnp.exp(lg[:, None] * (idx[None, :] + 1.0))
    k_dec = jnp.exp(lg[:, None] * (c - 1.0 - idx[None, :]))
    n = c * n_rep
    if n_rep > 1:
        eye = jnp.eye(n_rep, dtype=F32)
        dmask = jnp.einsum('ab,hij->haibj', eye, dmask).reshape(RET_HEADS, n, n)
        q_dec = jnp.tile(q_dec, (1, n_rep))
        k_dec = jnp.tile(k_dec, (1, n_rep))
    bcast = lambda d: jnp.broadcast_to(d[:, :, None], (RET_HEADS, n, RET_HEAD_DIM))
    gammas = [1.0 - 2.0 ** (-5.0 - h) for h in range(RET_HEADS)]
    chunk_dec = tuple(g ** c for g in gammas)
    return dmask, bcast(q_dec), bcast(k_dec), chunk_dec


def _gate_weights(wa, wx):
    per = LRU_GROUP // LRU_BLOCK_DIM
    eye = jnp.eye(per, dtype=wa.dtype)

    def grouped(w):
        w = w.reshape(LRU_BLOCKS // per, per, LRU_BLOCK_DIM, LRU_BLOCK_DIM)
        return jnp.einsum('ab,gakj->gakbj', eye, w).reshape(LRU_BLOCKS // per, LRU_GROUP, LRU_GROUP)

    return jnp.concatenate([grouped(wa), grouped(wx)], axis=-1).astype(BF16)


def kernel(x_prompt, x_sample, state_ret, state_lru_h, state_lru_conv, cache_mem_k, cache_mem_v, mem_prompt,
           ffn1_norm, ffn1_wg, ffn1_wu, ffn1_wd, mix_norm, w_in, ret_gn_gain, conv_w, conv_b,
           lru_wa, lru_ba, lru_wx, lru_bx, lru_lambda, lru_norm, w_out,
           xattn_norm, xattn_wq, xattn_wk, xattn_wv, xattn_wo,
           ffn2_norm, ffn2_wg, ffn2_wu, ffn2_wd, final_norm):
    depth = ffn1_wg.shape[0]
    assert depth == 1, "single-layer trunk"
    bp, tp, _ = x_prompt.shape
    bs, ts, _ = x_sample.shape
    assert ts == SUBLANES and tp % RET_CHUNK == 0
    bf = lambda w: w[0].astype(BF16)
    row = lambda v: v.reshape(1, -1)

    f1 = (row(ffn1_norm[0]), bf(ffn1_wg), bf(ffn1_wu), bf(ffn1_wd), row(mix_norm[0]), bf(w_in))
    f2 = (row(ffn2_norm[0]), bf(ffn2_wg), bf(ffn2_wu), bf(ffn2_wd), row(final_norm))
    small = [row(ret_gn_gain[0]), conv_w[0], row(conv_b[0]), _gate_weights(lru_wa[0], lru_wx[0]),
             row(lru_ba[0]), row(lru_bx[0]), row(lru_lambda[0]), row(lru_norm[0])]
    wout, xn, wq, wo = bf(w_out), row(xattn_norm[0]), bf(xattn_wq), bf(xattn_wo)

    tm = 256
    mk, mv, mk_bf, mv_bf = _mem_kv(mem_prompt, bf(xattn_wk), bf(xattn_wv))
    x1p, zp = _ffn_in(x_prompt.reshape(bp * tp, D_MODEL), *f1, 2 * tm)
    cos_p, sin_p = _rope_tables(jnp.arange(tp, dtype=jnp.int32))
    tt = 256
    dmask, qdec, kdec, cdec = _decay_tables(tt, 1)
    y_p, s_p, h_p, c_p = _mixer_ffn_prompt(
        zp.reshape(bp, tp, D_IN), x1p.reshape(bp, tp, D_MODEL), cos_p, sin_p,
        mk_bf, mv_bf, dmask, qdec, kdec, cdec, small, wout, xn, wq, wo, *f2, tt)

    g_seq = 16
    x1s, zs = _ffn_in(x_sample.reshape(bs * ts, D_MODEL), *f1, tm)
    cos_s, sin_s = _rope_tables(PAST_LEN + jnp.arange(ts, dtype=jnp.int32))
    cos_s, sin_s = jnp.tile(cos_s, (g_seq, 1)), jnp.tile(sin_s, (g_seq, 1))
    dmask8, qdec8, kdec8, cdec8 = _decay_tables(ts, g_seq)
    x2s, qs, s_s, h_s, c_s = _mixer_sample(
        zs, x1s, cos_s, sin_s, state_ret[0], state_lru_h[0].reshape(bs, 1, D_LRU), state_lru_conv[0],
        dmask8, qdec8, kdec8, cdec8, small, wout, xn, wq, g_seq)
    g_att = 8
    x3s = _xattn_sample(qs, x2s, cache_mem_k, cache_mem_v, wo, g_att)
    y_s = _ffn_out(x3s, *f2, tm).reshape(bs, ts, D_MODEL)

    return (y_p, y_s, s_p[None], h_p.reshape(1, bp, D_LRU), c_p[None],
            mk, mv, s_s[None], h_s.reshape(1, bs, D_LRU), c_s[None])
```

```python
import functools
import math

import jax
import jax.numpy as jnp
from jax import lax
from jax.experimental import pallas as pl
from jax.experimental.pallas import tpu as pltpu

D_MODEL = 1024
PAST_LEN = 16384
N_MEM = 256
D_RET = 512
D_LRU = 512
RET_HEADS = 4
RET_HEAD_DIM = 128
RET_CHUNK = 128
ROPE_BASE = 10000.0
LRU_BLOCKS = 8
LRU_BLOCK_DIM = 64
CONV_WIDTH = 4
LRU_C = 8.0
XATTN_HEADS = 4
XATTN_HEAD_DIM = 256
D_FF = 2816
D_IN = 3072
EPS = 1e-6

SUBLANES = 8
LANES = 128
FFN_CHUNK = 256
LRU_GROUP = 128
PROMPT_TRACE_ORDER = "FFMFMFFMFMFMFFMFMFFMFMFM"
VMEM_LIMIT_BYTES = 56 * 1024 * 1024

F32 = jnp.float32
BF16 = jnp.bfloat16


def _rms(x, g):
    ms = jnp.mean(x * x, axis=-1, keepdims=True)
    return x * lax.rsqrt(ms + EPS) * g


def _dot(a, b):
    return jnp.dot(a, b, preferred_element_type=F32)


def _swiglu(h_bf, wg_ref, wu_ref, wd_ref):
    g = _dot(h_bf, wg_ref[...])
    u = _dot(h_bf, wu_ref[...])
    a = (g * jax.nn.sigmoid(g)) * u
    return _dot(a.astype(BF16), wd_ref[...])


def _const_spec(shape):
    nd = len(shape)
    return pl.BlockSpec(shape, lambda *_: (0,) * nd)


def _params(n_axes):
    return pltpu.CompilerParams(dimension_semantics=("arbitrary",) * n_axes,
                                vmem_limit_bytes=VMEM_LIMIT_BYTES)


ZB_WIDTH = 4 * D_RET
ZF_WIDTH = D_RET + 2 * D_LRU


def _rotary(x, cos2, sin2):
    return x * cos2 + pltpu.roll(x, RET_HEAD_DIM // 2, axis=1) * sin2


def _ffn_in_kernel(x_ref, cs_ref, sn_ref, kdec_ref, n1_ref, wg_ref, wu_ref, wd_ref, n2_ref, win_ref,
                   x1_ref, zb_ref, zf_ref):
    x = x_ref[...]
    h = _rms(x, n1_ref[...]).astype(BF16)
    x1 = x + 0.5 * _swiglu(h, wg_ref, wu_ref, wd_ref)
    x1_ref[...] = x1
    hn = _rms(x1, n2_ref[...]).astype(BF16)
    z = _dot(hn, win_ref[...])
    cos2 = cs_ref[...]
    sin2 = sn_ref[...]
    for hd in range(RET_HEADS):
        head = lambda j: slice(j * D_RET + hd * RET_HEAD_DIM, j * D_RET + (hd + 1) * RET_HEAD_DIM)
        q = _rotary(z[:, head(0)], cos2, sin2)
        k = _rotary(z[:, head(1)], cos2, sin2) * (RET_HEAD_DIM ** -0.5)
        zb_ref[:, head(0)] = q.astype(BF16)
        zb_ref[:, head(1)] = k.astype(BF16)
        zb_ref[:, head(2)] = (k * kdec_ref[hd]).astype(BF16)
    zb_ref[:, 3 * D_RET:] = z[:, 2 * D_RET:3 * D_RET].astype(BF16)
    zf_ref[...] = z[:, 3 * D_RET:]


def _ffn_in(x, cos2, sin2, kdec, n1, wg, wu, wd, n2, win, tm):
    rows = x.shape[0]
    pos_tiles = cos2.shape[0] // tm
    row_spec = lambda w: pl.BlockSpec((tm, w), lambda i: (i, 0))
    tab_spec = pl.BlockSpec((tm, RET_HEAD_DIM), lambda i: (i % pos_tiles, 0))
    return pl.pallas_call(
        _ffn_in_kernel,
        grid=(rows // tm,),
        in_specs=[row_spec(D_MODEL), tab_spec, tab_spec, _const_spec(kdec.shape),
                  _const_spec(n1.shape), _const_spec(wg.shape), _const_spec(wu.shape),
                  _const_spec(wd.shape), _const_spec(n2.shape), _const_spec(win.shape)],
        out_specs=[row_spec(D_MODEL), row_spec(ZB_WIDTH), row_spec(ZF_WIDTH)],
        out_shape=[jax.ShapeDtypeStruct((rows, D_MODEL), F32), jax.ShapeDtypeStruct((rows, ZB_WIDTH), BF16),
                   jax.ShapeDtypeStruct((rows, ZF_WIDTH), F32)],
        compiler_params=_params(1),
        name="ffn_in",
    )(x, cos2, sin2, kdec, n1, wg, wu, wd, n2, win)


def _ffn_out_kernel(x_ref, n1_ref, wg_ref, wu_ref, wd_ref, nf_ref, y_ref):
    x = x_ref[...]
    h = _rms(x, n1_ref[...]).astype(BF16)
    x4 = x + 0.5 * _swiglu(h, wg_ref, wu_ref, wd_ref)
    y_ref[...] = _rms(x4, nf_ref[...])


def _ffn_out(x, n1, wg, wu, wd, nf, tm):
    rows = x.shape[0]
    row_spec = pl.BlockSpec((tm, D_MODEL), lambda i: (i, 0))
    return pl.pallas_call(
        _ffn_out_kernel,
        grid=(rows // tm,),
        in_specs=[row_spec, _const_spec(n1.shape), _const_spec(wg.shape), _const_spec(wu.shape),
                  _const_spec(wd.shape), _const_spec(nf.shape)],
        out_specs=row_spec,
        out_shape=jax.ShapeDtypeStruct((rows, D_MODEL), F32),
        compiler_params=_params(1),
        name="ffn_out",
    )(x, n1, wg, wu, wd, nf)


def _mem_kv_kernel(m_ref, wk_ref, wv_ref, k_ref, v_ref, kb_ref, vb_ref):
    m = m_ref[0].astype(BF16)
    for w_ref, o_ref, ob_ref in ((wk_ref, k_ref, kb_ref), (wv_ref, v_ref, vb_ref)):
        kv = _dot(m, w_ref[...])
        ob_ref[0] = kv.astype(BF16)
        for h in range(XATTN_HEADS):
            o_ref[:, h, :] = kv[:, h * XATTN_HEAD_DIM:(h + 1) * XATTN_HEAD_DIM]


def _mem_kv(mem, wk, wv):
    bsz = mem.shape[0]
    row_spec = pl.BlockSpec((1, N_MEM, D_MODEL), lambda b: (b, 0, 0))
    cache_spec = pl.BlockSpec((None, None, N_MEM, XATTN_HEADS, XATTN_HEAD_DIM), lambda b: (0, b, 0, 0, 0))
    cache_shape = jax.ShapeDtypeStruct((1, bsz, N_MEM, XATTN_HEADS, XATTN_HEAD_DIM), F32)
    return pl.pallas_call(
        _mem_kv_kernel,
        grid=(bsz,),
        in_specs=[row_spec, _const_spec(wk.shape), _const_spec(wv.shape)],
        out_specs=[cache_spec, cache_spec, row_spec, row_spec],
        out_shape=[cache_shape, cache_shape] + [jax.ShapeDtypeStruct((bsz, N_MEM, D_MODEL), BF16)] * 2,
        compiler_params=_params(1),
        name="mem_kv",
    )(mem, wk, wv)


def _group_norm_gate(o, gain, g):
    mu = jnp.mean(o, axis=-1, keepdims=True)
    oc = o - mu
    var = jnp.mean(oc * oc, axis=-1, keepdims=True)
    on = oc * lax.rsqrt(var + EPS) * gain
    return (g * jax.nn.sigmoid(g)) * on


def _softplus(x):
    return jnp.maximum(x, 0.0) + jnp.log1p(jnp.exp(-jnp.abs(x)))


def _lru_group(u3, hist3, gi, cw_ref, cb_ref, wgate_ref, ba_ref, bx_ref, lam_ref):
    cols = slice(gi * LRU_GROUP, (gi + 1) * LRU_GROUP)
    uc3 = _conv_taps(u3, hist3, cw_ref[:, cols], cb_ref[:, cols])
    uc = uc3.reshape(u3.shape[0] * SUBLANES, LRU_GROUP)
    pre = _dot(uc.astype(BF16), wgate_ref[gi])
    r = jax.nn.sigmoid(pre[:, :LRU_GROUP] + ba_ref[:, cols])
    i = jax.nn.sigmoid(pre[:, LRU_GROUP:] + bx_ref[:, cols])
    log_a = (-LRU_C) * r * _softplus(-lam_ref[:, cols])
    a = jnp.exp(log_a)
    gx = jnp.sqrt(1.0 - a * a) * (i * uc)
    return a.reshape(u3.shape), gx.reshape(u3.shape)


def _slab_scan(a, g):
    t = lax.broadcasted_iota(jnp.int32, a.shape, 1)
    d = 1
    while d < SUBLANES:
        keep = t >= d
        g_prev = jnp.where(keep, pltpu.roll(g, d, axis=1), 0.0)
        a_prev = jnp.where(keep, pltpu.roll(a, d, axis=1), 1.0)
        g = g + a * g_prev
        a = a * a_prev
        d *= 2
    return a, g


def _gelu_tanh(x):
    return 0.5 * x * (1.0 + jnp.tanh(math.sqrt(2.0 / math.pi) * (x + 0.044715 * (x * x * x))))


def _conv_taps(u3, hist3, cw, cb):
    t = lax.broadcasted_iota(jnp.int32, u3.shape, 1)
    acc = u3 * cw[CONV_WIDTH - 1:CONV_WIDTH, :] + cb
    for k in range(1, CONV_WIDTH):
        prev = jnp.where(t >= k, pltpu.roll(u3, k, axis=1), pltpu.roll(hist3, k, axis=1))
        acc = acc + prev * cw[CONV_WIDTH - 1 - k:CONV_WIDTH - k, :]
    return acc


def _attention_head(q_bf, k_bf, v_bf):
    sc = lax.dot_general(q_bf, k_bf, (((1,), (1,)), ((), ())), preferred_element_type=F32)
    p = jnp.exp(sc - jnp.max(sc, axis=-1, keepdims=True))
    l = jnp.sum(p, axis=-1, keepdims=True)
    return (_dot(p.astype(BF16), v_bf) / l).astype(BF16)


def _merge(pattern, ffn_units, mixer_units):
    assert pattern.count("F") == len(ffn_units) and pattern.count("M") == len(mixer_units)
    ffn_it, mixer_it = iter(ffn_units), iter(mixer_units)
    return [next(ffn_it) if p == "F" else next(mixer_it) for p in pattern]


def _mixer_ffn_prompt_kernel(zb_ref, zf_ref, x1_ref, mk_ref, mv_ref, dmask_ref, qdec_ref,
                             gn_ref, cw_ref, cb_ref, wgate_ref, ba_ref, bx_ref, lam_ref, ln_ref,
                             wout_ref, xn_ref, wq_ref, wo_ref, n2_ref, wg_ref, wu_ref, wd_ref, nf_ref,
                             y_ref, so_ref, ho_ref, cv_ref,
                             hist_ref, s_ref, h_ref, x3_ref,
                             *, tt, tiles_per_seq, n_tiles, chunk_dec):
    step = pl.program_id(0)
    t_idx = jnp.minimum(step, n_tiles - 1) % tiles_per_seq

    @pl.when(step == 0)
    def _():
        x3_ref[...] = jnp.zeros_like(x3_ref)

    @pl.when(t_idx == 0)
    def _():
        s_ref[...] = jnp.zeros_like(s_ref)
        h_ref[...] = jnp.zeros_like(h_ref)
        hist_ref[...] = jnp.zeros_like(hist_ref)

    x3_prev = x3_ref[...]
    hf = _rms(x3_prev, n2_ref[...]).astype(BF16)
    ffn = {"act": [], "x4": []}

    def ffn_up_chunk(j):
        cols = slice(j * FFN_CHUNK, (j + 1) * FFN_CHUNK)
        g = _dot(hf, wg_ref[:, cols])
        u = _dot(hf, wu_ref[:, cols])
        ffn["act"].append(((g * jax.nn.sigmoid(g)) * u).astype(BF16))

    def ffn_down_chunk(j):
        if j == 0:
            ffn["act"] = jnp.concatenate(ffn["act"], axis=-1)
        cols = slice(j * 2 * FFN_CHUNK, (j + 1) * 2 * FFN_CHUNK)
        ffn["x4"].append(x3_prev[:, cols] + 0.5 * _dot(ffn["act"], wd_ref[:, cols]))

    def ffn_finish():
        y_ref[0] = _rms(jnp.concatenate(ffn["x4"], axis=-1), nf_ref[...])

    ffn_units = [functools.partial(ffn_up_chunk, j) for j in range(D_FF // FFN_CHUNK)]
    ffn_units += [functools.partial(ffn_down_chunk, j) for j in range(D_MODEL // (2 * FFN_CHUNK))] + [ffn_finish]

    mx = {"ret": [], "hs": [], "hs_sq": [], "lru": []}

    def retention_unit(pair):
        heads = (2 * pair, 2 * pair + 1)
        col = lambda j, h: slice(j * D_RET + h * RET_HEAD_DIM, j * D_RET + (h + 1) * RET_HEAD_DIM)
        q_bf = [zb_ref[0, :, col(0, h)] for h in heads]
        v_bf = [zb_ref[0, :, col(3, h)] for h in heads]
        sc = [lax.dot_general(q_bf[i], zb_ref[0, :, col(1, h)], (((1,), (1,)), ((), ())),
                              preferred_element_type=F32) * dmask_ref[h] for i, h in enumerate(heads)]
        s_old = [s_ref[h] for h in heads]
        cross = [_dot(q_bf[i], s_old[i].astype(BF16)) * qdec_ref[h] for i, h in enumerate(heads)]
        o = [_dot(sc[i].astype(BF16), v_bf[i]) + cross[i] for i in range(2)]
        for i, h in enumerate(heads):
            s_ref[h] = chunk_dec[h] * s_old[i] + lax.dot_general(
                zb_ref[0, :, col(2, h)], v_bf[i], (((0,), (0,)), ((), ())), preferred_element_type=F32)
        for i, h in enumerate(heads):
            gain = gn_ref[:, h * RET_HEAD_DIM:(h + 1) * RET_HEAD_DIM]
            mx["ret"].append(_group_norm_gate(o[i], gain, zf_ref[0, :, col(0, h)]).astype(BF16))

    n_slab = tt // SUBLANES

    def lru_group_unit(gi):
        cols = slice(gi * LRU_GROUP, (gi + 1) * LRU_GROUP)
        u3 = zf_ref[0, :, D_RET + gi * LRU_GROUP:D_RET + (gi + 1) * LRU_GROUP].reshape(
            n_slab, SUBLANES, LRU_GROUP)
        hist3 = jnp.concatenate([hist_ref[:, cols].reshape(1, SUBLANES, LRU_GROUP), u3[:-1]], axis=0)
        hist_ref[:, cols] = u3[n_slab - 1]
        a3, gx3 = _lru_group(u3, hist3, gi, cw_ref, cb_ref, wgate_ref, ba_ref, bx_ref, lam_ref)
        a_cum, h_loc = _slab_scan(a3, gx3)
        carry = h_ref[:, cols]
        slabs = []
        for s in range(n_slab):
            hs_s = h_loc[s] + a_cum[s] * carry
            slabs.append(hs_s)
            carry = hs_s[SUBLANES - 1:, :]
        h_ref[:, cols] = carry
        hs = jnp.concatenate(slabs, axis=0)
        mx["hs"].append(hs)
        mx["hs_sq"].append(jnp.sum(hs * hs, axis=-1, keepdims=True))

    def lru_norm_unit():
        inv = lax.rsqrt(sum(mx["hs_sq"]) * (1.0 / D_LRU) + EPS)
        for gi, hs in enumerate(mx["hs"]):
            cols = slice(gi * LRU_GROUP, (gi + 1) * LRU_GROUP)
            gate = zf_ref[0, :, D_RET + D_LRU + gi * LRU_GROUP:D_RET + D_LRU + (gi + 1) * LRU_GROUP]
            mx["lru"].append((hs * inv * ln_ref[:, cols] * _gelu_tanh(gate)).astype(BF16))

    def out_proj_unit():
        mix = jnp.concatenate(mx["ret"] + mx["lru"], axis=-1)
        x2 = x1_ref[0] + _dot(mix, wout_ref[...])
        hx = _rms(x2, xn_ref[...]).astype(BF16)
        mx["x2"] = x2
        mx["q"] = (_dot(hx, wq_ref[...]) * (XATTN_HEAD_DIM ** -0.5)).astype(BF16)
        mx["o"] = []

    def attention_unit(pair):
        sls = [slice(h * XATTN_HEAD_DIM, (h + 1) * XATTN_HEAD_DIM) for h in (2 * pair, 2 * pair + 1)]
        sc = [lax.dot_general(mx["q"][:, sl], mk_ref[0, :, sl], (((1,), (1,)), ((), ())),
                              preferred_element_type=F32) for sl in sls]
        p = [jnp.exp(s - jnp.max(s, axis=-1, keepdims=True)) for s in sc]
        l = [jnp.sum(pi, axis=-1, keepdims=True) for pi in p]
        o = [_dot(pi.astype(BF16), mv_ref[0, :, sl]) for pi, sl in zip(p, sls)]
        mx["o"] += [(oi / li).astype(BF16) for oi, li in zip(o, l)]

    def attn_out_unit():
        x3_ref[...] = mx["x2"] + _dot(jnp.concatenate(mx["o"], axis=-1), wo_ref[...])

    mixer_units = [functools.partial(retention_unit, p) for p in range(RET_HEADS // 2)]
    mixer_units += [functools.partial(lru_group_unit, gi) for gi in range(D_LRU // LRU_GROUP)]
    mixer_units += [lru_norm_unit, out_proj_unit]
    mixer_units += [functools.partial(attention_unit, p) for p in range(XATTN_HEADS // 2)]

    for unit in _merge(PROMPT_TRACE_ORDER, ffn_units, mixer_units):
        unit()
    attn_out_unit()

    @pl.when(jnp.logical_and(t_idx == tiles_per_seq - 1, step < n_tiles))
    def _():
        so_ref[0] = s_ref[...]
        ho_ref[0] = h_ref[...]
        cv_ref[0] = hist_ref[SUBLANES - (CONV_WIDTH - 1):, :]


def _mixer_ffn_prompt(zb, zf, x1, mk, mv, dmask, qdec, chunk_dec, small, wout, xn, wq, wo,
                      n2, wg, wu, wd, nf, tt):
    bsz, seq, _ = x1.shape
    tps = seq // tt
    n_tiles = bsz * tps
    consts = [dmask, qdec, *small, wout, xn, wq, wo, n2, wg, wu, wd, nf]
    cur = lambda s: jnp.minimum(s, n_tiles - 1)
    prev = lambda s: jnp.maximum(s - 1, 0)
    seq_spec = lambda w: pl.BlockSpec((1, tt, w), lambda s: (cur(s) // tps, cur(s) % tps, 0))
    mem_spec = pl.BlockSpec((1, N_MEM, D_MODEL), lambda s: (cur(s) // tps, 0, 0))
    state_spec = lambda *tail: pl.BlockSpec((1, *tail), lambda s: (cur(s) // tps,) + (0,) * len(tail))
    return pl.pallas_call(
        functools.partial(_mixer_ffn_prompt_kernel, tt=tt, tiles_per_seq=tps, n_tiles=n_tiles,
                          chunk_dec=chunk_dec),
        grid=(n_tiles + 1,),
        in_specs=[seq_spec(ZB_WIDTH), seq_spec(ZF_WIDTH), seq_spec(D_MODEL), mem_spec, mem_spec]
        + [_const_spec(c.shape) for c in consts],
        out_specs=[pl.BlockSpec((1, tt, D_MODEL), lambda s: (prev(s) // tps, prev(s) % tps, 0)),
                   state_spec(RET_HEADS, RET_HEAD_DIM, RET_HEAD_DIM), state_spec(1, D_LRU),
                   state_spec(CONV_WIDTH - 1, D_LRU)],
        out_shape=[jax.ShapeDtypeStruct((bsz, seq, D_MODEL), F32),
                   jax.ShapeDtypeStruct((bsz, RET_HEADS, RET_HEAD_DIM, RET_HEAD_DIM), F32),
                   jax.ShapeDtypeStruct((bsz, 1, D_LRU), F32),
                   jax.ShapeDtypeStruct((bsz, CONV_WIDTH - 1, D_LRU), F32)],
        scratch_shapes=[pltpu.VMEM((SUBLANES, D_LRU), F32),
                        pltpu.VMEM((RET_HEADS, RET_HEAD_DIM, RET_HEAD_DIM), F32), pltpu.VMEM((1, D_LRU), F32),
                        pltpu.VMEM((tt, D_MODEL), F32)],
        compiler_params=_params(1),
        name="mixer_ffn_prompt",
    )(zb, zf, x1, mk, mv, *consts)


def _mixer_sample_kernel(zb_ref, zf_ref, x1_ref, s0_ref, h0_ref, c0_ref, dmask_ref, qdec_ref,
                         gn_ref, cw_ref, cb_ref, wgate_ref, ba_ref, bx_ref, lam_ref, ln_ref,
                         wout_ref, xn_ref, wq_ref,
                         x2_ref, q_ref, s_ref, h_ref, cv_ref,
                         mix_ref, qs_ref, kd_ref, vs_ref, cross_ref, hist_ref, *, g_seq, chunk_dec):
    rows_n = g_seq * SUBLANES
    col = lambda j, h: slice(j * D_RET + h * RET_HEAD_DIM, j * D_RET + (h + 1) * RET_HEAD_DIM)
    for h in range(RET_HEADS):
        qs_ref[h] = zb_ref[:, col(0, h)].astype(F32)
        kd_ref[h] = zb_ref[:, col(2, h)].astype(F32)
        vs_ref[h] = zb_ref[:, col(3, h)].astype(F32)

    def seq_body(s, carry):
        r0 = pl.multiple_of(s * SUBLANES, SUBLANES)
        for h in range(RET_HEADS):
            s_old = s0_ref[s, h]
            q_bf = qs_ref[h, pl.ds(r0, SUBLANES), :].astype(BF16)
            cross_ref[h, pl.ds(r0, SUBLANES), :] = _dot(q_bf, s_old.astype(BF16))
            kd_bf = kd_ref[h, pl.ds(r0, SUBLANES), :].astype(BF16)
            v_bf = vs_ref[h, pl.ds(r0, SUBLANES), :].astype(BF16)
            s_ref[s, h] = chunk_dec[h] * s_old + lax.dot_general(
                kd_bf, v_bf, (((0,), (0,)), ((), ())), preferred_element_type=F32)
        return carry

    lax.fori_loop(0, g_seq, seq_body, 0, unroll=8)

    for h in range(RET_HEADS):
        sc = lax.dot_general(zb_ref[:, col(0, h)], zb_ref[:, col(1, h)], (((1,), (1,)), ((), ())),
                             preferred_element_type=F32) * dmask_ref[h]
        o = _dot(sc.astype(BF16), zb_ref[:, col(3, h)]) + cross_ref[h] * qdec_ref[h]
        gain = gn_ref[:, h * RET_HEAD_DIM:(h + 1) * RET_HEAD_DIM]
        mix_ref[:, h * RET_HEAD_DIM:(h + 1) * RET_HEAD_DIM] = _group_norm_gate(
            o, gain, zf_ref[:, col(0, h)]).astype(BF16)

    hist_ref[...] = jnp.zeros_like(hist_ref)
    hist_ref[:, SUBLANES - (CONV_WIDTH - 1):, :] = c0_ref[...]
    hs_groups = []
    for gi in range(D_LRU // LRU_GROUP):
        cols = slice(gi * LRU_GROUP, (gi + 1) * LRU_GROUP)
        u3 = zf_ref[:, D_RET + gi * LRU_GROUP:D_RET + (gi + 1) * LRU_GROUP].reshape(
            g_seq, SUBLANES, LRU_GROUP)
        cv_ref[:, :, cols] = u3[:, SUBLANES - (CONV_WIDTH - 1):, :]
        a3, gx3 = _lru_group(u3, hist_ref[:, :, cols], gi, cw_ref, cb_ref, wgate_ref, ba_ref, bx_ref, lam_ref)
        a_cum, h_loc = _slab_scan(a3, gx3)
        hs3 = h_loc + a_cum * h0_ref[:, :, cols]
        h_ref[:, :, cols] = hs3[:, SUBLANES - 1:, :]
        hs_groups.append(hs3.reshape(rows_n, LRU_GROUP))
    hs = jnp.concatenate(hs_groups, axis=-1)
    gate = zf_ref[:, D_RET + D_LRU:]
    mix_ref[:, D_RET:] = (_rms(hs, ln_ref[...]) * _gelu_tanh(gate)).astype(BF16)

    x2 = x1_ref[...] + _dot(mix_ref[...], wout_ref[...])
    x2_ref[...] = x2
    hx = _rms(x2, xn_ref[...]).astype(BF16)
    q = (_dot(hx, wq_ref[...]) * (XATTN_HEAD_DIM ** -0.5)).reshape(g_seq, SUBLANES, D_MODEL)
    q_heads = [q[:, :, h * XATTN_HEAD_DIM:(h + 1) * XATTN_HEAD_DIM] for h in range(XATTN_HEADS)]
    q_ref[...] = jnp.concatenate(q_heads, axis=1).reshape(rows_n * XATTN_HEADS, XATTN_HEAD_DIM).astype(BF16)


def _mixer_sample(zb, zf, x1, s0, h0, c0, dmask, qdec, chunk_dec, small, wout, xn, wq, g_seq):
    rows = x1.shape[0]
    n_seq = rows // SUBLANES
    rn = g_seq * SUBLANES
    consts = [dmask, qdec, *small, wout, xn, wq]
    row_spec = lambda w: pl.BlockSpec((rn, w), lambda i: (i, 0))
    st_spec = pl.BlockSpec((g_seq, RET_HEADS, RET_HEAD_DIM, RET_HEAD_DIM), lambda i: (i, 0, 0, 0))
    h_spec = pl.BlockSpec((g_seq, 1, D_LRU), lambda i: (i, 0, 0))
    c_spec = pl.BlockSpec((g_seq, CONV_WIDTH - 1, D_LRU), lambda i: (i, 0, 0))
    head_buf = pltpu.VMEM((RET_HEADS, rn, RET_HEAD_DIM), F32)
    return pl.pallas_call(
        functools.partial(_mixer_sample_kernel, g_seq=g_seq, chunk_dec=chunk_dec),
        grid=(n_seq // g_seq,),
        in_specs=[row_spec(ZB_WIDTH), row_spec(ZF_WIDTH), row_spec(D_MODEL), st_spec, h_spec, c_spec]
        + [_const_spec(c.shape) for c in consts],
        out_specs=[row_spec(D_MODEL), pl.BlockSpec((rn * XATTN_HEADS, XATTN_HEAD_DIM), lambda i: (i, 0)),
                   st_spec, h_spec, c_spec],
        out_shape=[jax.ShapeDtypeStruct((rows, D_MODEL), F32),
                   jax.ShapeDtypeStruct((rows * XATTN_HEADS, XATTN_HEAD_DIM), BF16),
                   jax.ShapeDtypeStruct(s0.shape, F32), jax.ShapeDtypeStruct(h0.shape, F32),
                   jax.ShapeDtypeStruct(c0.shape, F32)],
        scratch_shapes=[pltpu.VMEM((rn, D_MODEL), BF16), head_buf, head_buf, head_buf, head_buf,
                        pltpu.VMEM((g_seq, SUBLANES, D_LRU), F32)],
        compiler_params=_params(1),
        name="mixer_sample",
    )(zb, zf, x1, s0, h0, c0, *consts)


def _xattn_sample_kernel(q_ref, x2_ref, k_ref, v_ref, wo_ref, x3_ref, o_ref, *, g_seq):
    n_q = XATTN_HEADS * SUBLANES
    n_kv = N_MEM * XATTN_HEADS
    col_head = lax.broadcasted_iota(jnp.int32, (n_q, n_kv), 1) % XATTN_HEADS
    row_head = lax.broadcasted_iota(jnp.int32, (n_q, n_kv), 0) // SUBLANES
    own_head = col_head == row_head
    for s in range(g_seq):
        k_all = k_ref[s].reshape(n_kv, XATTN_HEAD_DIM).astype(BF16)
        v_all = v_ref[s].reshape(n_kv, XATTN_HEAD_DIM).astype(BF16)
        sc = lax.dot_general(q_ref[s * n_q:(s + 1) * n_q, :], k_all, (((1,), (1,)), ((), ())),
                             preferred_element_type=F32)
        sc = jnp.where(own_head, sc, -jnp.inf)
        p = jnp.exp(sc - jnp.max(sc, axis=-1, keepdims=True))
        o = _dot(p.astype(BF16), v_all) / jnp.sum(p, axis=-1, keepdims=True)
        for h in range(XATTN_HEADS):
            o_ref[s * SUBLANES:(s + 1) * SUBLANES, h * XATTN_HEAD_DIM:(h + 1) * XATTN_HEAD_DIM] = (
                o[h * SUBLANES:(h + 1) * SUBLANES, :])
    x3_ref[...] = x2_ref[...] + _dot(o_ref[...].astype(BF16), wo_ref[...])


def _xattn_sample(q, x2, ck, cv, wo, g_seq):
    rows = x2.shape[0]
    rn = g_seq * SUBLANES
    row_spec = pl.BlockSpec((rn, D_MODEL), lambda i: (i, 0))
    q_spec = pl.BlockSpec((rn * XATTN_HEADS, XATTN_HEAD_DIM), lambda i: (i, 0))
    mem_spec = pl.BlockSpec((None, g_seq, N_MEM, XATTN_HEADS, XATTN_HEAD_DIM), lambda i: (0, i, 0, 0, 0))
    return pl.pallas_call(
        functools.partial(_xattn_sample_kernel, g_seq=g_seq),
        grid=(rows // rn,),
        in_specs=[q_spec, row_spec, mem_spec, mem_spec, _const_spec(wo.shape)],
        out_specs=row_spec,
        out_shape=jax.ShapeDtypeStruct((rows, D_MODEL), F32),
        scratch_shapes=[pltpu.VMEM((rn, D_MODEL), F32)],
        compiler_params=_params(1),
        name="xattn_sample",
    )(q, x2, ck, cv, wo)


def _rope_tables(pos):
    half = RET_HEAD_DIM // 2
    inv = ROPE_BASE ** (-jnp.arange(half, dtype=F32) / half)
    ang = pos.astype(F32)[:, None] * inv[None, :]
    cos, sin = jnp.cos(ang), jnp.sin(ang)
    return jnp.concatenate([cos, cos], axis=-1), jnp.concatenate([-sin, sin], axis=-1)


def _decay_tables(c, n_rep):
    lg = jnp.log(1.0 - 2.0 ** (-5.0 - jnp.arange(RET_HEADS, dtype=F32)))
    idx = jnp.arange(c, dtype=F32)
    diff = idx[:, None] - idx[None, :]
    dmask = jnp.where(diff[None] >= 0, jnp.exp(lg[:, None, None] * jnp.maximum(diff, 0.0)[None]), 0.0)
    q_dec = jnp.exp(lg[:, None] * (idx[None, :] + 1.0))
    k_dec = jnp.exp(lg[:, None] * (c - 1.0 - idx[None, :]))
    n = c * n_rep
    if n_rep > 1:
        eye = jnp.eye(n_rep, dtype=F32)
        dmask = jnp.einsum('ab,hij->haibj', eye, dmask).reshape(RET_HEADS, n, n)
        q_dec = jnp.tile(q_dec, (1, n_rep))
        k_dec = jnp.tile(k_dec, (1, n_rep))
    bcast = lambda d: jnp.broadcast_to(d[:, :, None], (RET_HEADS, n, RET_HEAD_DIM))
    gammas = [1.0 - 2.0 ** (-5.0 - h) for h in range(RET_HEADS)]
    chunk_dec = tuple(g ** c for g in gammas)
    return dmask, bcast(q_dec), bcast(k_dec), chunk_dec


def _gate_weights(wa, wx):
    per = LRU_GROUP // LRU_BLOCK_DIM
    eye = jnp.eye(per, dtype=wa.dtype)

    def grouped(w):
        w = w.reshape(LRU_BLOCKS // per, per, LRU_BLOCK_DIM, LRU_BLOCK_DIM)
        return jnp.einsum('ab,gakj->gakbj', eye, w).reshape(LRU_BLOCKS // per, LRU_GROUP, LRU_GROUP)

    return jnp.concatenate([grouped(wa), grouped(wx)], axis=-1).astype(BF16)


def kernel(x_prompt, x_sample, state_ret, state_lru_h, state_lru_conv, cache_mem_k, cache_mem_v, mem_prompt,
           ffn1_norm, ffn1_wg, ffn1_wu, ffn1_wd, mix_norm, w_in, ret_gn_gain, conv_w, conv_b,
           lru_wa, lru_ba, lru_wx, lru_bx, lru_lambda, lru_norm, w_out,
           xattn_norm, xattn_wq, xattn_wk, xattn_wv, xattn_wo,
           ffn2_norm, ffn2_wg, ffn2_wu, ffn2_wd, final_norm):
    depth = ffn1_wg.shape[0]
    assert depth == 1, "single-layer trunk"
    bp, tp, _ = x_prompt.shape
    bs, ts, _ = x_sample.shape
    assert ts == SUBLANES and tp % RET_CHUNK == 0
    bf = lambda w: w[0].astype(BF16)
    row = lambda v: v.reshape(1, -1)

    f1 = (row(ffn1_norm[0]), bf(ffn1_wg), bf(ffn1_wu), bf(ffn1_wd), row(mix_norm[0]), bf(w_in))
    f2 = (row(ffn2_norm[0]), bf(ffn2_wg), bf(ffn2_wu), bf(ffn2_wd), row(final_norm))
    small = [row(ret_gn_gain[0]), conv_w[0], row(conv_b[0]), _gate_weights(lru_wa[0], lru_wx[0]),
             row(lru_ba[0]), row(lru_bx[0]), row(lru_lambda[0]), row(lru_norm[0])]
    wout, xn, wq, wo = bf(w_out), row(xattn_norm[0]), bf(xattn_wq), bf(xattn_wo)

    tm = 256
    mk, mv, mk_bf, mv_bf = _mem_kv(mem_prompt, bf(xattn_wk), bf(xattn_wv))
    tt = 256
    tm_p = 2 * tm
    cos_p, sin_p = _rope_tables(jnp.arange(tp, dtype=jnp.int32))
    dmask, qdec, kdec, cdec = _decay_tables(tt, 1)
    x1p, zbp, zfp = _ffn_in(x_prompt.reshape(bp * tp, D_MODEL), cos_p, sin_p,
                            jnp.tile(kdec, (1, tm_p // tt, 1)), *f1, tm_p)
    y_p, s_p, h_p, c_p = _mixer_ffn_prompt(
        zbp.reshape(bp, tp, ZB_WIDTH), zfp.reshape(bp, tp, ZF_WIDTH), x1p.reshape(bp, tp, D_MODEL),
        mk_bf, mv_bf, dmask, qdec, cdec, small, wout, xn, wq, wo, *f2, tt)

    g_seq = 16
    cos_s, sin_s = _rope_tables(PAST_LEN + jnp.arange(ts, dtype=jnp.int32))
    cos_s, sin_s = jnp.tile(cos_s, (tm // ts, 1)), jnp.tile(sin_s, (tm // ts, 1))
    kdec8 = _decay_tables(ts, tm // ts)[2]
    x1s, zbs, zfs = _ffn_in(x_sample.reshape(bs * ts, D_MODEL), cos_s, sin_s, kdec8, *f1, tm)
    dmask8, qdec8, _, cdec8 = _decay_tables(ts, g_seq)
    x2s, qs, s_s, h_s, c_s = _mixer_sample(
        zbs, zfs, x1s, state_ret[0], state_lru_h[0].reshape(bs, 1, D_LRU), state_lru_conv[0],
        dmask8, qdec8, cdec8, small, wout, xn, wq, g_seq)
    g_att = 8
    x3s = _xattn_sample(qs, x2s, cache_mem_k, cache_mem_v, wo, g_att)
    y_s = _ffn_out(x3s, *f2, tm).reshape(bs, ts, D_MODEL)

    return (y_p, y_s, s_p[None], h_p.reshape(1, bp, D_LRU), c_p[None],
            mk, mv, s_s[None], h_s.reshape(1, bs, D_LRU), c_s[None])
```
